```python
import math
import jax, jax.numpy as jnp
from jax import lax
import numpy as np

D_MODEL = 1024
BATCH = 8
SEQ = 4096
DEPTH = 2

HEAD_DIM = 64
A_HEADS = 8
A_KV = 2
A_REP = A_HEADS // A_KV
A_WINDOW = 128
A_BLOCK = 128
B_HEADS = 8
B_KV = 2
B_REP = B_HEADS // B_KV
CMP_LEN = 32
CMP_STRIDE = 16
CMP_HIDDEN = 256
SEL_BLOCK = 64
SEL_TOPN = 16
SEL_Q_CHUNK = 64
B_WINDOW = 512
B_WIN_BLOCK = 128
N_BUCKETS = 32
MAX_EXACT = 16
MAX_DIST = 128
TOTAL_HEADS = A_HEADS + B_HEADS
N_GROUPS = 4
EXPERTS_PER_GROUP = 8
N_EXPERTS = N_GROUPS * EXPERTS_PER_GROUP
TOP_K_IN_GROUP = 2
EXPERT_FF = 512
MOE_BLOCK = 128

RMS_EPS = 1e-5
NEG = -1e30
FORCE = 1e9

A_QW = A_HEADS * HEAD_DIM
A_KVW = A_KV * HEAD_DIM
B_QW = B_HEADS * HEAD_DIM
B_KVW = B_KV * HEAD_DIM
IN_SIZES = (A_QW, A_KVW, A_KVW, B_QW, B_KVW, B_KVW, B_KVW, B_KVW, B_KVW, B_KVW, 3 * B_HEADS, 2 * D_MODEL)
IN_COLS = sum(IN_SIZES)

kernel_name = "hybrid_swa_sink_nsa_hier_moe"


def rmsnorm(x, g):
    xf = x.astype(jnp.float32)
    y = xf * lax.rsqrt(jnp.mean(xf * xf, axis=-1, keepdims=True) + RMS_EPS)
    return y.astype(x.dtype) * g


def t5_bucket(dist):
    n = jnp.maximum(dist, 0)
    nf = jnp.maximum(n, 1).astype(jnp.float32)
    large = MAX_EXACT + (jnp.log(nf / MAX_EXACT) / math.log(MAX_DIST / MAX_EXACT)
                         * (N_BUCKETS - MAX_EXACT)).astype(jnp.int32)
    large = jnp.minimum(large, N_BUCKETS - 1)
    return jnp.where(n < MAX_EXACT, n, large)


def softmax_with_sink(s, sink):
    sink = sink.astype(jnp.float32)
    m = jnp.maximum(jnp.max(s, axis=-1, keepdims=True), sink)
    e = jnp.exp(s - m)
    return e / (jnp.sum(e, axis=-1, keepdims=True) + jnp.exp(sink - m))


def banded_attention(q, k, v, window, block, bias_tab, sinks):
    B, S, G, R, Dh = q.shape
    nb = S // block
    npv = -(-(window - 1) // block)
    L = (npv + 1) * block
    qb = q.reshape(B, nb, block, G, R, Dh)
    pad = ((0, 0), (npv * block, 0), (0, 0), (0, 0))
    kp = jnp.pad(k, pad)
    vp = jnp.pad(v, pad)
    kidx = jnp.arange(nb)[:, None] * block + jnp.arange(L)[None, :]
    kb = kp[:, kidx]
    vb = vp[:, kidx]
    qpos = jnp.arange(nb)[:, None] * block + jnp.arange(block)[None, :]
    kpos = kidx - npv * block
    dist = qpos[:, :, None] - kpos[:, None, :]
    valid = (dist >= 0) & (dist < window) & (kpos[:, None, :] >= 0)
    bias = bias_tab[t5_bucket(dist)].reshape(nb, block, L, G, R)
    bias = jnp.transpose(bias, (3, 4, 0, 1, 2)).astype(jnp.float32)
    s = jnp.einsum('bnqgrd,bnkgd->bgrnqk', qb, kb).astype(jnp.float32) * (Dh ** -0.5) + bias
    s = jnp.where(valid, s, NEG)
    if sinks is None:
        p = jax.nn.softmax(s, axis=-1)
    else:
        p = softmax_with_sink(s, sinks.reshape(G, R, 1, 1, 1))
    o = jnp.einsum('bgrnqk,bnkgd->bnqgrd', p.astype(v.dtype), vb)
    return o.reshape(B, S, G * R * Dh)


def nsa_compressed_selected(q, kc, vc, ks, vs, pos_k, w1_k, w2_k, pos_v, w1_v, w2_v, bias_tab):
    B, S, G, R, Dh = q.shape
    nc = (S - CMP_LEN) // CMP_STRIDE + 1
    ns = S // SEL_BLOCK
    n_sel = min(SEL_TOPN, ns)
    cidx = jnp.arange(nc)[:, None] * CMP_STRIDE + jnp.arange(CMP_LEN)[None, :]

    def compress(t, pos, w1, w2):
        blocks = t[:, cidx] + pos[:, None, :]
        blocks = jnp.transpose(blocks, (0, 3, 1, 2, 4)).reshape(B, G, nc, CMP_LEN * Dh)
        return jax.nn.gelu(blocks @ w1) @ w2

    kcmp = compress(kc, pos_k, w1_k, w2_k)
    vcmp = compress(vc, pos_v, w1_v, w2_v)
    cstart = jnp.arange(nc) * CMP_STRIDE
    cend = cstart + CMP_LEN - 1
    sstart = jnp.arange(ns) * SEL_BLOCK
    overlap = ((cstart[:, None] < sstart[None, :] + SEL_BLOCK)
               & (cstart[:, None] + CMP_LEN > sstart[None, :])).astype(jnp.float32)
    ksb = jnp.transpose(ks.reshape(B, ns, SEL_BLOCK, G, Dh), (0, 3, 1, 2, 4))
    vsb = jnp.transpose(vs.reshape(B, ns, SEL_BLOCK, G, Dh), (0, 3, 1, 2, 4))
    tab = jnp.transpose(bias_tab.reshape(N_BUCKETS, G, R), (1, 0, 2))
    nq = S // SEL_Q_CHUNK
    qch = jnp.moveaxis(q.reshape(B, nq, SEL_Q_CHUNK, G, R, Dh), 1, 0)
    bi = jnp.arange(B)[:, None, None, None]
    gi = jnp.arange(G)[None, :, None, None]
    scale = Dh ** -0.5
    jsel = jnp.arange(ns)

    def chunk(args):
        qc, c = args
        t = c * SEL_Q_CHUNK + jnp.arange(SEL_Q_CHUNK)
        sc = jnp.einsum('bqgrd,bgnd->bgrqn', qc, kcmp).astype(jnp.float32) * scale
        cvalid = cend[None, :] <= t[:, None]
        pc = jax.nn.softmax(jnp.where(cvalid, sc, NEG), axis=-1) * cvalid
        o_cmp = jnp.einsum('bgrqn,bgnd->bqgrd', pc.astype(vcmp.dtype), vcmp)
        imp = jnp.einsum('bgrqn,nj->bgqj', pc, overlap)
        tb = t // SEL_BLOCK
        forced = (jsel[None, :] == 0) | (jsel[None, :] == tb[:, None]) | (jsel[None, :] == tb[:, None] - 1)
        causal = sstart[None, :] <= t[:, None]
        imp = jnp.where(causal, jnp.where(forced, FORCE, imp), NEG)
        top_val, top_idx = lax.top_k(imp, n_sel)
        sel_ok = top_val > NEG * 0.5
        kg = ksb[bi, gi, top_idx]
        vg = vsb[bi, gi, top_idx]
        kpos = top_idx[..., None] * SEL_BLOCK + jnp.arange(SEL_BLOCK)
        dist = t[:, None, None] - kpos
        svalid = sel_ok[..., None] & (dist >= 0)
        bias = tab[gi[..., None], t5_bucket(dist)]
        ss = (jnp.einsum('bqgrd,bgqnkd->bgrqnk', qc, kg).astype(jnp.float32) * scale
              + jnp.moveaxis(bias, -1, 2).astype(jnp.float32))
        ss = jnp.where(svalid[:, :, None], ss, NEG)
        ps = jax.nn.softmax(ss.reshape(B, G, R, SEL_Q_CHUNK, n_sel * SEL_BLOCK), axis=-1)
        ps = ps.reshape(B, G, R, SEL_Q_CHUNK, n_sel, SEL_BLOCK)
        o_slc = jnp.einsum('bgrqnk,bgqnkd->bqgrd', ps.astype(vg.dtype), vg)
        return o_cmp, o_slc

    o_cmp, o_slc = lax.map(chunk, (qch, jnp.arange(nq)))
    o_cmp = jnp.moveaxis(o_cmp, 0, 1).reshape(B, S, G * R * Dh)
    o_slc = jnp.moveaxis(o_slc, 0, 1).reshape(B, S, G * R * Dh)
    return o_cmp, o_slc


def hybrid_mixer(xn, w_in, a_sinks, cmp_pos_k, cmp_w1_k, cmp_w2_k, cmp_pos_v, cmp_w1_v, cmp_w2_v,
                 w_br_a, w_br_b, w_out, rel_bias):
    B, S, _ = xn.shape
    proj = xn @ w_in
    offs = np.cumsum(IN_SIZES)[:-1].tolist()
    aq, ak, av, bq, bkc, bvc, bks, bvs, bkw, bvw, bg, mg = jnp.split(proj, offs, axis=-1)
    o_a = banded_attention(aq.reshape(B, S, A_KV, A_REP, HEAD_DIM),
                           ak.reshape(B, S, A_KV, HEAD_DIM), av.reshape(B, S, A_KV, HEAD_DIM),
                           A_WINDOW, A_BLOCK, rel_bias[:, :A_HEADS], a_sinks)
    qb = bq.reshape(B, S, B_KV, B_REP, HEAD_DIM)
    kv = lambda t: t.reshape(B, S, B_KV, HEAD_DIM)
    bias_b = rel_bias[:, A_HEADS:]
    o_cmp, o_slc = nsa_compressed_selected(qb, kv(bkc), kv(bvc), kv(bks), kv(bvs),
                                           cmp_pos_k, cmp_w1_k, cmp_w2_k, cmp_pos_v, cmp_w1_v, cmp_w2_v,
                                           bias_b)
    o_win = banded_attention(qb, kv(bkw), kv(bvw), B_WINDOW, B_WIN_BLOCK, bias_b, None)
    gates = jax.nn.sigmoid(bg.astype(jnp.float32)).reshape(B, S, B_HEADS, 3)
    heads = lambda o: o.reshape(B, S, B_HEADS, HEAD_DIM)
    o_b = (gates[..., 0:1] * heads(o_cmp) + gates[..., 1:2] * heads(o_slc)
           + gates[..., 2:3] * heads(o_win)).astype(xn.dtype).reshape(B, S, B_QW)
    g_a, g_b = jnp.split(jax.nn.sigmoid(mg), 2, axis=-1)
    merged = g_a * (o_a @ w_br_a) + g_b * (o_b @ w_br_b)
    return merged @ w_out


def hier_moe(xn, w_group, b_group, w_expert, b_expert, w1, w3, w2):
    B, S, D = xn.shape
    T = B * S
    xt = xn.reshape(T, D)
    glog = (xt @ w_group).astype(jnp.float32) + b_group.astype(jnp.float32)
    gprob = jax.nn.softmax(glog, axis=-1)
    gsel = jnp.argmax(glog, axis=-1).astype(jnp.int32)
    gw = jnp.take_along_axis(gprob, gsel[:, None], axis=-1)
    elog = ((xt @ w_expert).astype(jnp.float32) + b_expert.astype(jnp.float32)).reshape(T, N_GROUPS, EXPERTS_PER_GROUP)
    elog_g = jnp.take_along_axis(elog, gsel[:, None, None], axis=1)[:, 0]
    ev, ei = lax.top_k(elog_g, TOP_K_IN_GROUP)
    ew = jax.nn.softmax(ev, axis=-1) * gw
    eid = gsel[:, None] * EXPERTS_PER_GROUP + ei.astype(jnp.int32)
    A = T * TOP_K_IN_GROUP
    e_flat = eid.reshape(A)
    w_flat = ew.reshape(A)
    tok = jnp.repeat(jnp.arange(T, dtype=jnp.int32), TOP_K_IN_GROUP)
    order = jnp.argsort(e_flat)
    e_s, tok_s, w_s = e_flat[order], tok[order], w_flat[order]
    counts = jnp.zeros((N_EXPERTS,), jnp.int32).at[e_flat].add(1)
    starts = jnp.cumsum(counts) - counts
    padded = (counts + MOE_BLOCK - 1) // MOE_BLOCK * MOE_BLOCK
    pstarts = jnp.cumsum(padded) - padded
    dest = pstarts[e_s] + (jnp.arange(A, dtype=jnp.int32) - starts[e_s])
    P = A + N_EXPERTS * MOE_BLOCK
    nblk = P // MOE_BLOCK
    buf_tok = jnp.full((P,), T, jnp.int32).at[dest].set(tok_s)
    buf_w = jnp.zeros((P,), jnp.float32).at[dest].set(w_s)
    blk_expert = jnp.clip(jnp.searchsorted(pstarts + padded, jnp.arange(nblk) * MOE_BLOCK, side='right'),
                          0, N_EXPERTS - 1)
    x_pad = jnp.concatenate([xt, jnp.zeros((1, D), xt.dtype)], axis=0)
    xin = x_pad[buf_tok].reshape(nblk, MOE_BLOCK, D)

    def expert_block(args):
        xb, e = args
        return (jax.nn.silu(xb @ w1[e]) * (xb @ w3[e])) @ w2[e]

    y = lax.map(expert_block, (xin, blk_expert)).reshape(P, D)
    out = jax.ops.segment_sum(y.astype(jnp.float32) * buf_w[:, None], buf_tok, num_segments=T + 1)[:T]
    return out.astype(xn.dtype).reshape(B, S, D)


def setup_inputs(seed: int = 0) -> dict:
    key = jax.random.key(seed)
    keys = iter(jax.random.split(key, 24))
    nrm = lambda shape, scale: jax.random.normal(next(keys), shape, jnp.float32) * scale
    L = DEPTH
    return {
        "x": nrm((BATCH, SEQ, D_MODEL), 1.0),
        "rel_bias": nrm((N_BUCKETS, TOTAL_HEADS), 0.5),
        "norm_mix": 1.0 + nrm((L, D_MODEL), 0.1),
        "w_in": nrm((L, D_MODEL, IN_COLS), D_MODEL ** -0.5),
        "a_sinks": nrm((L, A_HEADS), 0.5),
        "cmp_pos_k": nrm((L, CMP_LEN, HEAD_DIM), 0.5),
        "cmp_w1_k": nrm((L, CMP_LEN * HEAD_DIM, CMP_HIDDEN), (CMP_LEN * HEAD_DIM) ** -0.5),
        "cmp_w2_k": nrm((L, CMP_HIDDEN, HEAD_DIM), CMP_HIDDEN ** -0.5),
        "cmp_pos_v": nrm((L, CMP_LEN, HEAD_DIM), 0.5),
        "cmp_w1_v": nrm((L, CMP_LEN * HEAD_DIM, CMP_HIDDEN), (CMP_LEN * HEAD_DIM) ** -0.5),
        "cmp_w2_v": nrm((L, CMP_HIDDEN, HEAD_DIM), CMP_HIDDEN ** -0.5),
        "w_br_a": nrm((L, A_QW, D_MODEL), A_QW ** -0.5),
        "w_br_b": nrm((L, B_QW, D_MODEL), B_QW ** -0.5),
        "w_out": nrm((L, D_MODEL, D_MODEL), D_MODEL ** -0.5),
        "norm_ffn": 1.0 + nrm((L, D_MODEL), 0.1),
        "w_group": nrm((L, D_MODEL, N_GROUPS), D_MODEL ** -0.5),
        "b_group": nrm((L, N_GROUPS), 0.01),
        "w_expert": nrm((L, D_MODEL, N_EXPERTS), D_MODEL ** -0.5),
        "b_expert": nrm((L, N_EXPERTS), 0.01),
        "w1": nrm((L, N_EXPERTS, D_MODEL, EXPERT_FF), D_MODEL ** -0.5),
        "w3": nrm((L, N_EXPERTS, D_MODEL, EXPERT_FF), D_MODEL ** -0.5),
        "w2": nrm((L, N_EXPERTS, EXPERT_FF, D_MODEL), EXPERT_FF ** -0.5),
        "norm_final": 1.0 + nrm((D_MODEL,), 0.1),
    }


def reference(x, rel_bias, norm_mix, w_in, a_sinks, cmp_pos_k, cmp_w1_k, cmp_w2_k, cmp_pos_v, cmp_w1_v, cmp_w2_v,
              w_br_a, w_br_b, w_out, norm_ffn, w_group, b_group, w_expert, b_expert, w1, w3, w2, norm_final):
    h = x
    for l in range(DEPTH):
        hn = rmsnorm(h, norm_mix[l])
        h = h + hybrid_mixer(hn, w_in[l], a_sinks[l], cmp_pos_k[l], cmp_w1_k[l], cmp_w2_k[l],
                             cmp_pos_v[l], cmp_w1_v[l], cmp_w2_v[l], w_br_a[l], w_br_b[l], w_out[l], rel_bias)
        hn = rmsnorm(h, norm_ffn[l])
        h = h + hier_moe(hn, w_group[l], b_group[l], w_expert[l], b_expert[l], w1[l], w3[l], w2[l])
    return rmsnorm(h, norm_final)
```

```python
import functools
import math

import numpy as np
import jax
import jax.numpy as jnp
from jax import lax
from jax.experimental import pallas as pl
from jax.experimental.pallas import tpu as pltpu

F32 = jnp.float32
BF16 = jnp.bfloat16
I32 = jnp.int32

D_MODEL = 1024
HEAD_DIM = 64
N_HEADS = 8
N_KV = 2
N_REP = 4
A_WINDOW = 128
B_WINDOW = 512
ATT_BLOCK = 128
CMP_LEN = 32
CMP_STRIDE = 16
CMP_HIDDEN = 256
SEL_BLOCK = 64
SEL_TOPN = 16
N_BUCKETS = 32
MAX_EXACT = 16
MAX_DIST = 128
N_GROUPS = 4
EXPERTS_PER_GROUP = 8
N_EXPERTS = 32
EXPERT_FF = 512
RMS_EPS = 1e-5
NEG = -1e30
FORCE = 1e9
SEL_MASK = -1e9

QKV_W = 2048
GATE_W = 2176
BG_OFF = 2048
COL_AK, COL_AV, COL_BKC, COL_BVC, COL_BKS, COL_BVS, COL_BKW, COL_BVW = 8, 9, 10, 11, 12, 13, 14, 15

ROW_TILE = 512
BAND_TQ = 512
SEL_TQ = 256
MOE_TM = 512
VMEM_LIMIT = 56 * 1024 * 1024

_NT = (((1,), (1,)), ((), ()))


def _cparams(sem):
    return pltpu.CompilerParams(dimension_semantics=sem, vmem_limit_bytes=VMEM_LIMIT)


def _rms(h, g):
    ms = jnp.mean(h * h, axis=-1, keepdims=True)
    return (h * lax.rsqrt(ms + RMS_EPS)) * g


def _sigmoid(z):
    return 1.0 / (1.0 + jnp.exp(-z))


def _stack_heads(ref, r0, r1, c0):
    return jnp.concatenate(
        [ref[r0:r1, c0 + r * HEAD_DIM:c0 + (r + 1) * HEAD_DIM] for r in range(N_REP)], axis=0)


def _unstack_heads(o, n):
    return jnp.concatenate([o[r * n:(r + 1) * n] for r in range(N_REP)], axis=1)


def _inproj_body(with_moe, *refs):
    if with_moe:
        h_ref, m0_ref, m1_ref, g_ref, w_ref, hout_ref, qkv_ref, gate_ref = refs
        h = h_ref[...] + m0_ref[...] + m1_ref[...]
        hout_ref[...] = h
    else:
        h_ref, g_ref, w_ref, qkv_ref, gate_ref = refs
        h = h_ref[...]
    xb = _rms(h, g_ref[...]).astype(BF16)
    for c0 in range(0, QKV_W, 512):
        acc = jnp.dot(xb, w_ref[:, c0:c0 + 512], preferred_element_type=F32)
        qkv_ref[:, c0:c0 + 512] = acc.astype(BF16)
    for c0 in range(0, GATE_W, 512):
        c1 = min(c0 + 512, GATE_W)
        z = jnp.dot(xb, w_ref[:, QKV_W + c0:QKV_W + c1], preferred_element_type=F32)
        gate_ref[:, c0:c1] = _sigmoid(z).astype(BF16)


def _inproj(h, moe, gain, w_p):
    T = h.shape[0]
    tm = ROW_TILE
    nt = T // tm
    row = lambda i: (i, 0)
    const = lambda i: (0, 0)
    in_specs = [pl.BlockSpec((tm, D_MODEL), row)]
    args = [h]
    out_shape = []
    out_specs = []
    if moe is not None:
        in_specs += [pl.BlockSpec((tm, D_MODEL), row),
                     pl.BlockSpec((tm, D_MODEL), lambda i: (i + nt, 0))]
        args += [moe, moe]
        out_shape.append(jax.ShapeDtypeStruct((T, D_MODEL), F32))
        out_specs.append(pl.BlockSpec((tm, D_MODEL), row))
    in_specs += [pl.BlockSpec((1, D_MODEL), const),
                 pl.BlockSpec((D_MODEL, QKV_W + GATE_W), const)]
    args += [gain, w_p]
    out_shape += [jax.ShapeDtypeStruct((T, QKV_W), BF16), jax.ShapeDtypeStruct((T, GATE_W), BF16)]
    out_specs += [pl.BlockSpec((tm, QKV_W), row), pl.BlockSpec((tm, GATE_W), row)]
    res = pl.pallas_call(
        functools.partial(_inproj_body, moe is not None),
        grid=(nt,), in_specs=in_specs, out_specs=out_specs, out_shape=out_shape,
        compiler_params=_cparams(("parallel",)), name="inproj")(*args)
    if moe is None:
        return h, res[0], res[1]
    return res[0], res[1], res[2]


def _compress_body(x_ref, pos_ref, w1_ref, w2_ref, o_ref):
    half = CMP_STRIDE * HEAD_DIM
    for g in range(N_KV):
        x = x_ref[0, 0, g].astype(F32)
        lo = (x + pos_ref[0, 0:1, :]).astype(BF16)
        hi = (x + pos_ref[0, 1:2, :]).astype(BF16)
        a = jnp.dot(lo, w1_ref[0, 0:half, :], preferred_element_type=F32)
        b = jnp.dot(hi, w1_ref[0, half:2 * half, :], preferred_element_type=F32)
        n = b.shape[0]
        hsum = a + pltpu.roll(b, n - 1, 0)
        hid = jax.nn.gelu(hsum, approximate=True).astype(BF16)
        o_ref[0, 0, g] = jnp.dot(hid, w2_ref[0], preferred_element_type=F32).astype(BF16)


def _compress(xr, pos, w1, w2):
    _, B, G, nr, _ = xr.shape
    return pl.pallas_call(
        _compress_body,
        grid=(2, B),
        in_specs=[pl.BlockSpec((1, 1, G, nr, CMP_STRIDE * HEAD_DIM), lambda k, b: (k, b, 0, 0, 0)),
                  pl.BlockSpec((1, 2, CMP_STRIDE * HEAD_DIM), lambda k, b: (k, 0, 0)),
                  pl.BlockSpec((1, CMP_LEN * HEAD_DIM, CMP_HIDDEN), lambda k, b: (k, 0, 0)),
                  pl.BlockSpec((1, CMP_HIDDEN, HEAD_DIM), lambda k, b: (k, 0, 0))],
        out_specs=pl.BlockSpec((1, 1, G, nr, HEAD_DIM), lambda k, b: (k, b, 0, 0, 0)),
        out_shape=jax.ShapeDtypeStruct((2, B, G, nr, HEAD_DIM), BF16),
        compiler_params=_cparams(("parallel", "parallel")), name="compress")(xr, pos, w1, w2)


def _cmpsel_body(q_ref, kc_ref, vc_ref, ovt_ref, ocmp_ref, selm_ref, imp_scr):
    i = pl.program_id(1)
    tq = q_ref.shape[0]
    nc = kc_ref.shape[3]
    ns = ovt_ref.shape[0]
    t0 = i * tq
    tcol = t0 + lax.broadcasted_iota(I32, (tq, 1), 0)
    ncol = lax.broadcasted_iota(I32, (1, nc), 1)
    cval = (ncol * CMP_STRIDE + (CMP_LEN - 1)) <= tcol
    cval4 = jnp.concatenate([cval] * N_REP, axis=0)
    jrow = lax.broadcasted_iota(I32, (ns, 1), 0)
    trow = t0 + lax.broadcasted_iota(I32, (1, tq), 1)
    tb = lax.shift_right_logical(trow, 6)
    forced = (jrow == 0) | (jrow == tb) | (jrow == tb - 1)
    causal = (jrow * SEL_BLOCK) <= trow
    n_sel = float(min(SEL_TOPN, ns))
    for g in range(N_KV):
        qs = _stack_heads(q_ref, 0, tq, g * N_REP * HEAD_DIM)
        s = lax.dot_general(qs, kc_ref[0, 0, g], _NT, preferred_element_type=F32)
        sm = jnp.where(cval4, s, NEG)
        m = jnp.max(sm, axis=-1, keepdims=True)
        e = jnp.where(cval4, jnp.exp(sm - m), 0.0)
        den = jnp.sum(e, axis=-1, keepdims=True)
        pc = e / jnp.where(den > 0.0, den, 1.0)
        o = jnp.dot(pc.astype(BF16), vc_ref[0, 0, g], preferred_element_type=F32)
        ocmp_ref[:, g * 256:(g + 1) * 256] = _unstack_heads(o, tq).astype(BF16)
        pcs = pc[0:tq] + pc[tq:2 * tq] + pc[2 * tq:3 * tq] + pc[3 * tq:4 * tq]
        hi = pcs.astype(BF16)
        lo = (pcs - hi.astype(F32)).astype(BF16)
        imp = (lax.dot_general(ovt_ref[...], hi, _NT, preferred_element_type=F32)
               + lax.dot_general(ovt_ref[...], lo, _NT, preferred_element_type=F32))
        imp = jnp.where(causal, jnp.where(forced, FORCE, imp), NEG)
        imp_scr[...] = imp

        def rank_step(ii, cnt):
            rowv = imp_scr[pl.ds(ii, 1), :]
            beats = (rowv > imp) | ((rowv == imp) & (jrow > ii))
            return cnt + jnp.where(beats, 1.0, 0.0)

        cnt = lax.fori_loop(0, ns, rank_step, jnp.zeros((ns, tq), F32))
        sel = (cnt < n_sel) & (imp > NEG * 0.5)
        mt = jnp.where(sel, 0.0, SEL_MASK)
        if ns < 128:
            mt = jnp.concatenate([mt, jnp.zeros((128 - ns, tq), F32)], axis=0)
        selm_ref[:, g * 128:(g + 1) * 128] = mt.T.astype(BF16)


def _cmpsel(qkv, kcmp, vcmp, ovt, B, S):
    T = B * S
    tq = SEL_TQ
    nq = S // tq
    nc = kcmp.shape[3]
    ns = ovt.shape[0]
    return pl.pallas_call(
        _cmpsel_body,
        grid=(B, nq),
        in_specs=[pl.BlockSpec((tq, 512), lambda b, i: (b * nq + i, 1)),
                  pl.BlockSpec((1, 1, N_KV, nc, HEAD_DIM), lambda b, i: (0, b, 0, 0, 0)),
                  pl.BlockSpec((1, 1, N_KV, nc, HEAD_DIM), lambda b, i: (1, b, 0, 0, 0)),
                  pl.BlockSpec((ns, nc), lambda b, i: (0, 0))],
        out_specs=[pl.BlockSpec((tq, 512), lambda b, i: (b * nq + i, 0)),
                   pl.BlockSpec((tq, 256), lambda b, i: (b * nq + i, 0))],
        out_shape=[jax.ShapeDtypeStruct((T, 512), BF16), jax.ShapeDtypeStruct((T, 256), BF16)],
        scratch_shapes=[pltpu.VMEM((ns, tq), F32)],
        compiler_params=_cparams(("parallel", "parallel")), name="cmpsel")(qkv, kcmp, vcmp, ovt)


def _slc_body(q_ref, selm_ref, ka_ref, v_ref, bias_ref, o_ref, qa_scr, m_scr, l_scr, acc_scr):
    i = pl.program_id(1)
    tq = q_ref.shape[0]
    prev = jnp.maximum(i - 1, 0)
    col = lax.broadcasted_iota(I32, (1, 2 * tq), 1)
    first_mask = jnp.logical_and(col < tq, i == 0)

    def update(s, v):
        m_old = m_scr[...]
        m_new = jnp.maximum(m_old, jnp.max(s, axis=-1, keepdims=True))
        alpha = jnp.exp(m_old - m_new)
        p = jnp.exp(s - m_new)
        l_scr[...] = alpha * l_scr[...] + jnp.sum(p, axis=-1, keepdims=True)
        acc_scr[...] = alpha * acc_scr[...] + jnp.dot(p.astype(BF16), v, preferred_element_type=F32)
        m_scr[...] = m_new

    for g in range(N_KV):
        sm = selm_ref[:, g * 128:g * 128 + SEL_BLOCK]
        for r in range(N_REP):
            c0 = g * 256 + r * HEAD_DIM
            qa_scr[r * tq:(r + 1) * tq, :] = jnp.concatenate([q_ref[:, c0:c0 + HEAD_DIM], sm], axis=1)
        m_scr[...] = jnp.full(m_scr.shape, NEG, F32)
        l_scr[...] = jnp.zeros(l_scr.shape, F32)
        acc_scr[...] = jnp.zeros(acc_scr.shape, F32)

        def far(c, carry):
            off = pl.multiple_of(c * tq, tq)
            kt = ka_ref[pl.ds(off, tq), g * 128:(g + 1) * 128]
            vt = v_ref[pl.ds(off, tq), g * HEAD_DIM:(g + 1) * HEAD_DIM]
            s = lax.dot_general(qa_scr[...], kt, _NT, preferred_element_type=F32)
            update(s, vt)
            return carry

        lax.fori_loop(0, prev, far, 0)
        poff = pl.multiple_of(prev * tq, tq)
        coff = pl.multiple_of(i * tq, tq)
        kn = jnp.concatenate([ka_ref[pl.ds(poff, tq), g * 128:(g + 1) * 128],
                              ka_ref[pl.ds(coff, tq), g * 128:(g + 1) * 128]], axis=0)
        vn = jnp.concatenate([v_ref[pl.ds(poff, tq), g * HEAD_DIM:(g + 1) * HEAD_DIM],
                              v_ref[pl.ds(coff, tq), g * HEAD_DIM:(g + 1) * HEAD_DIM]], axis=0)
        s = lax.dot_general(qa_scr[...], kn, _NT, preferred_element_type=F32) + bias_ref[g]
        s = jnp.where(first_mask, NEG, s)
        update(s, vn)
        o = acc_scr[...] / l_scr[...]
        o_ref[:, g * 256:(g + 1) * 256] = _unstack_heads(o, tq).astype(BF16)


def _slc(qkv, selm, kaug, bias_near, B, S):
    T = B * S
    tq = SEL_TQ
    nq = S // tq
    nsb = S // tq
    return pl.pallas_call(
        _slc_body,
        grid=(B, nq),
        in_specs=[pl.BlockSpec((tq, 512), lambda b, i: (b * nq + i, 1)),
                  pl.BlockSpec((tq, 256), lambda b, i: (b * nq + i, 0)),
                  pl.BlockSpec((S, 256), lambda b, i: (b, 0)),
                  pl.BlockSpec((S, 128), lambda b, i: (b, COL_BVS)),
                  pl.BlockSpec((N_KV, N_REP * tq, 2 * tq), lambda b, i: (0, 0, 0))],
        out_specs=pl.BlockSpec((tq, 512), lambda b, i: (b * nq + i, 0)),
        out_shape=jax.ShapeDtypeStruct((T, 512), BF16),
        scratch_shapes=[pltpu.VMEM((N_REP * tq, 128), BF16),
                        pltpu.VMEM((N_REP * tq, 1), F32),
                        pltpu.VMEM((N_REP * tq, 1), F32),
                        pltpu.VMEM((N_REP * tq, HEAD_DIM), F32)],
        compiler_params=_cparams(("parallel", "arbitrary")), name="slc")(qkv, selm, kaug, qkv, bias_near)


def _band_body(npv, has_sink, *refs):
    if has_sink:
        q_ref, kp_ref, km_ref, vp_ref, vm_ref, bias_ref, sink_ref, o_ref = refs
    else:
        q_ref, kp_ref, km_ref, vp_ref, vm_ref, bias_ref, o_ref = refs
    i = pl.program_id(1)
    blk = ATT_BLOCK
    L = (npv + 1) * blk
    nsub = q_ref.shape[0] // blk
    col = lax.broadcasted_iota(I32, (1, L), 1)
    for g in range(N_KV):
        ks = slice(g * HEAD_DIM, (g + 1) * HEAD_DIM)
        kfull = jnp.concatenate([kp_ref[:, ks], km_ref[:, ks]], axis=0)
        vfull = jnp.concatenate([vp_ref[:, ks], vm_ref[:, ks]], axis=0)
        if has_sink:
            sink = jnp.concatenate(
                [jnp.full((blk, 1), sink_ref[g * N_REP + r], F32) for r in range(N_REP)], axis=0)
        for sub in range(nsub):
            qs = _stack_heads(q_ref, sub * blk, (sub + 1) * blk, g * N_REP * HEAD_DIM)
            s = lax.dot_general(qs, kfull[sub * blk:sub * blk + L], _NT,
                                preferred_element_type=F32) + bias_ref[g]
            ncut = (npv - sub) * blk
            if ncut > 0:
                s = jnp.where(jnp.logical_and(col < ncut, i == 0), NEG, s)
            m = jnp.max(s, axis=-1, keepdims=True)
            if has_sink:
                m = jnp.maximum(m, sink)
            e = jnp.exp(s - m)
            den = jnp.sum(e, axis=-1, keepdims=True)
            if has_sink:
                den = den + jnp.exp(sink - m)
            o = jnp.dot(e.astype(BF16), vfull[sub * blk:sub * blk + L],
                        preferred_element_type=F32) / den
            o_ref[sub * blk:(sub + 1) * blk, g * 256:(g + 1) * 256] = _unstack_heads(o, blk).astype(BF16)


def _banded(qkv, bias, sinks, qcol, kcol, vcol, window, B, S):
    T = B * S
    tq = BAND_TQ
    nq = S // tq
    npv = -(-(window - 1) // ATT_BLOCK)
    pv = npv * ATT_BLOCK
    L = pv + ATT_BLOCK
    ratio = tq // pv
    prev_map = lambda c: (lambda b, i: (b * (S // pv) + jnp.maximum(i * ratio - 1, 0), c * (128 // 128)))
    main_map = lambda c: (lambda b, i: (b * nq + i, c))
    in_specs = [pl.BlockSpec((tq, 512), lambda b, i: (b * nq + i, qcol)),
                pl.BlockSpec((pv, 128), prev_map(kcol)), pl.BlockSpec((tq, 128), main_map(kcol)),
                pl.BlockSpec((pv, 128), prev_map(vcol)), pl.BlockSpec((tq, 128), main_map(vcol)),
                pl.BlockSpec((N_KV, N_REP * ATT_BLOCK, L), lambda b, i: (0, 0, 0))]
    args = [qkv, qkv, qkv, qkv, qkv, bias]
    if sinks is not None:
        in_specs.append(pl.BlockSpec(memory_space=pltpu.SMEM))
        args.append(sinks)
    return pl.pallas_call(
        functools.partial(_band_body, npv, sinks is not None),
        grid=(B, nq), in_specs=in_specs,
        out_specs=pl.BlockSpec((tq, 512), lambda b, i: (b * nq + i, 0)),
        out_shape=jax.ShapeDtypeStruct((T, 512), BF16),
        compiler_params=_cparams(("parallel", "parallel")),
        name="band_sink" if sinks is not None else "band_win")(*args)


def _out_body(h_ref, oa_ref, oc_ref, os_ref, ow_ref, gate_ref, ex_ref, wa_ref, wb_ref, wo_ref,
              gn_ref, wrh_ref, wrl_ref, br_ref, hout_ref, rinfo_ref):
    bgs = gate_ref[:, BG_OFF:BG_OFF + 128]
    ob = (jnp.dot(bgs, ex_ref[0], preferred_element_type=F32) * oc_ref[...].astype(F32)
          + jnp.dot(bgs, ex_ref[1], preferred_element_type=F32) * os_ref[...].astype(F32)
          + jnp.dot(bgs, ex_ref[2], preferred_element_type=F32) * ow_ref[...].astype(F32))
    ta = jnp.dot(oa_ref[...], wa_ref[...], preferred_element_type=F32)
    tb = jnp.dot(ob.astype(BF16), wb_ref[...], preferred_element_type=F32)
    merged = (gate_ref[:, 0:D_MODEL].astype(F32) * ta
              + gate_ref[:, D_MODEL:2 * D_MODEL].astype(F32) * tb)
    hn = h_ref[...] + jnp.dot(merged.astype(BF16), wo_ref[...], preferred_element_type=F32)
    hout_ref[...] = hn
    xn = _rms(hn, gn_ref[...])
    xh = xn.astype(BF16)
    xl = (xn - xh.astype(F32)).astype(BF16)
    logits = (jnp.dot(xh, wrh_ref[...], preferred_element_type=F32)
              + jnp.dot(xl, wrh_ref[...], preferred_element_type=F32)
              + jnp.dot(xh, wrl_ref[...], preferred_element_type=F32)) + br_ref[...]
    tm = logits.shape[0]
    lane = lax.broadcasted_iota(I32, (1, 128), 1).astype(F32)
    big = 1e9
    is_g = lane < N_GROUPS
    glog = jnp.where(is_g, logits, NEG)
    gmax = jnp.max(glog, axis=-1, keepdims=True)
    gsel = jnp.min(jnp.where(glog == gmax, lane, big), axis=-1, keepdims=True)
    gsum = jnp.sum(jnp.where(is_g, jnp.exp(logits - gmax), 0.0), axis=-1, keepdims=True)
    gw = 1.0 / gsum
    e_lo = N_GROUPS + gsel * EXPERTS_PER_GROUP
    in_g = (lane >= e_lo) & (lane < e_lo + EXPERTS_PER_GROUP)
    ev = jnp.where(in_g, logits, NEG)
    v1 = jnp.max(ev, axis=-1, keepdims=True)
    i1 = jnp.min(jnp.where(ev == v1, lane, big), axis=-1, keepdims=True)
    ev2 = jnp.where(lane == i1, NEG, ev)
    v2 = jnp.max(ev2, axis=-1, keepdims=True)
    i2 = jnp.min(jnp.where(ev2 == v2, lane, big), axis=-1, keepdims=True)
    d = jnp.exp(v2 - v1)
    p1 = 1.0 / (1.0 + d)
    p2 = d / (1.0 + d)
    lane8 = lax.broadcasted_iota(I32, (tm, 8), 1)
    rinfo_ref[...] = jnp.where(lane8 == 0, i1 - N_GROUPS,
                     jnp.where(lane8 == 1, i2 - N_GROUPS,
                     jnp.where(lane8 == 2, p1 * gw,
                     jnp.where(lane8 == 3, p2 * gw, 0.0))))


def _outproj(h, oa, oc, osl, ow, gate, ex, wa, wb, wo, gn, wrh, wrl, br):
    T = h.shape[0]
    tm = ROW_TILE
    row = lambda i: (i, 0)
    c2 = lambda i: (0, 0)
    c3 = lambda i: (0, 0, 0)
    return pl.pallas_call(
        _out_body,
        grid=(T // tm,),
        in_specs=[pl.BlockSpec((tm, D_MODEL), row),
                  pl.BlockSpec((tm, 512), row), pl.BlockSpec((tm, 512), row),
                  pl.BlockSpec((tm, 512), row), pl.BlockSpec((tm, 512), row),
                  pl.BlockSpec((tm, GATE_W), row),
                  pl.BlockSpec((3, 128, 512), c3),
                  pl.BlockSpec((512, D_MODEL), c2), pl.BlockSpec((512, D_MODEL), c2),
                  pl.BlockSpec((D_MODEL, D_MODEL), c2),
                  pl.BlockSpec((1, D_MODEL), c2),
                  pl.BlockSpec((D_MODEL, 128), c2), pl.BlockSpec((D_MODEL, 128), c2),
                  pl.BlockSpec((1, 128), c2)],
        out_specs=[pl.BlockSpec((tm, D_MODEL), row), pl.BlockSpec((tm, 8), row)],
        out_shape=[jax.ShapeDtypeStruct((T, D_MODEL), F32), jax.ShapeDtypeStruct((T, 8), F32)],
        compiler_params=_cparams(("parallel",)), name="outproj")(
            h, oa, oc, osl, ow, gate, ex, wa, wb, wo, gn, wrh, wrl, br)


def _moe_body(be_ref, nu_ref, gtok_ref, gtokn_ref, gdst_ref, wrow_ref, gn_ref, w1_ref, w3_ref, w2_ref,
              h_hbm, out_hbm, xbuf, ybuf, gsem, ssem):
    i = pl.program_id(0)
    nu = nu_ref[0]
    tm = xbuf.shape[1]
    slot = lax.rem(i, 2)

    def gather_copy(idx_ref, sl, r):
        t = idx_ref[0, 0, r]
        return pltpu.make_async_copy(h_hbm.at[pl.ds(t, 1)], xbuf.at[sl, pl.ds(r, 1)], gsem.at[sl])

    def scatter_copy(r):
        d = gdst_ref[0, 0, r]
        return pltpu.make_async_copy(ybuf.at[pl.ds(r, 1)], out_hbm.at[pl.ds(d, 1)], ssem.at[0])

    def start_gather(idx_ref, sl):
        def body(r, c):
            gather_copy(idx_ref, sl, r).start()
            return c
        lax.fori_loop(0, tm, body, 0)

    @pl.when(i == 0)
    def _():
        start_gather(gtok_ref, 0)
        ybuf[...] = jnp.zeros(ybuf.shape, F32)
        fill = pltpu.make_async_copy(ybuf, out_hbm.at[pl.ds(out_hbm.shape[0] - tm, tm)], ssem.at[0])
        fill.start()
        fill.wait()

    @pl.when(i + 1 < nu)
    def _():
        start_gather(gtokn_ref, 1 - slot)

    @pl.when(i < nu)
    def _():
        def wbody(r, c):
            gather_copy(gtok_ref, slot, r).wait()
            return c
        lax.fori_loop(0, tm, wbody, 0)
        x = _rms(xbuf[slot], gn_ref[...]).astype(BF16)
        h1 = jnp.dot(x, w1_ref[0], preferred_element_type=F32)
        h3 = jnp.dot(x, w3_ref[0], preferred_element_type=F32)
        act = (h1 * _sigmoid(h1) * h3).astype(BF16)
        y = jnp.dot(act, w2_ref[0], preferred_element_type=F32)
        ybuf[...] = y * wrow_ref[...]

        def sbody(r, c):
            scatter_copy(r).start()
            return c
        lax.fori_loop(0, tm, sbody, 0)

        def swbody(r, c):
            scatter_copy(r).wait()
            return c
        lax.fori_loop(0, tm, swbody, 0)


def _moe(h, blk_e, nused, gtok, gdst, wrow, gn, w1, w3, w2):
    T = h.shape[0]
    tm = MOE_TM
    nblk = gtok.shape[0]
    grid_spec = pltpu.PrefetchScalarGridSpec(
        num_scalar_prefetch=2,
        grid=(nblk,),
        in_specs=[pl.BlockSpec((1, 1, tm), lambda i, be, nu: (i, 0, 0), memory_space=pltpu.SMEM),
                  pl.BlockSpec((1, 1, tm), lambda i, be, nu: (jnp.minimum(i + 1, nblk - 1), 0, 0),
                               memory_space=pltpu.SMEM),
                  pl.BlockSpec((1, 1, tm), lambda i, be, nu: (i, 0, 0), memory_space=pltpu.SMEM),
                  pl.BlockSpec((tm, 1), lambda i, be, nu: (i, 0)),
                  pl.BlockSpec((1, D_MODEL), lambda i, be, nu: (0, 0)),
                  pl.BlockSpec((1, D_MODEL, EXPERT_FF), lambda i, be, nu: (be[i], 0, 0)),
                  pl.BlockSpec((1, D_MODEL, EXPERT_FF), lambda i, be, nu: (be[i], 0, 0)),
                  pl.BlockSpec((1, EXPERT_FF, D_MODEL), lambda i, be, nu: (be[i], 0, 0)),
                  pl.BlockSpec(memory_space=pl.ANY)],
        out_specs=pl.BlockSpec(memory_space=pl.ANY),
        scratch_shapes=[pltpu.VMEM((2, tm, D_MODEL), F32),
                        pltpu.VMEM((tm, D_MODEL), F32),
                        pltpu.SemaphoreType.DMA((2,)),
                        pltpu.SemaphoreType.DMA((1,))])
    return pl.pallas_call(
        _moe_body, grid_spec=grid_spec,
        out_shape=jax.ShapeDtypeStruct((2 * T + tm, D_MODEL), F32),
        compiler_params=_cparams(("arbitrary",)), name="moe")(
            blk_e, nused, gtok, gtok, gdst, wrow, gn, w1, w3, w2, h)


def _route(rinfo, T):
    tm = MOE_TM
    A = 2 * T
    e_flat = rinfo[:, 0:2].astype(I32).T.reshape(A)
    w_flat = rinfo[:, 2:4].T.reshape(A)
    onehot = (e_flat[:, None] == jnp.arange(N_EXPERTS, dtype=I32)[None, :]).astype(I32)
    csum = jnp.cumsum(onehot, axis=0)
    counts = csum[-1]
    rank = jnp.take_along_axis(csum, e_flat[:, None], axis=1)[:, 0] - 1
    padded = (counts + tm - 1) // tm * tm
    pend = jnp.cumsum(padded)
    pstart = pend - padded
    dest = pstart[e_flat] + rank
    P = A + N_EXPERTS * tm
    nblk = P // tm
    a = jnp.arange(A, dtype=I32)
    tok = jnp.where(a >= T, a - T, a)
    gtok = jnp.zeros((P,), I32).at[dest].set(tok)
    gdst = (A + jnp.arange(P, dtype=I32) % tm).at[dest].set(a)
    wrow = jnp.zeros((P,), F32).at[dest].set(w_flat)
    blk_e = jnp.clip(jnp.searchsorted(pend, jnp.arange(nblk, dtype=I32) * tm, side='right'),
                     0, N_EXPERTS - 1).astype(I32)
    nused = (pend[-1] // tm).astype(I32).reshape(1)
    return blk_e, nused, gtok.reshape(nblk, 1, tm), gdst.reshape(nblk, 1, tm), wrow.reshape(P, 1)


def _final_body(h_ref, m0_ref, m1_ref, g_ref, o_ref):
    o_ref[...] = _rms(h_ref[...] + m0_ref[...] + m1_ref[...], g_ref[...])


def _final(h, moe, gain):
    T = h.shape[0]
    tm = ROW_TILE
    nt = T // tm
    row = lambda i: (i, 0)
    return pl.pallas_call(
        _final_body, grid=(nt,),
        in_specs=[pl.BlockSpec((tm, D_MODEL), row), pl.BlockSpec((tm, D_MODEL), row),
                  pl.BlockSpec((tm, D_MODEL), lambda i: (i + nt, 0)),
                  pl.BlockSpec((1, D_MODEL), lambda i: (0, 0))],
        out_specs=pl.BlockSpec((tm, D_MODEL), row),
        out_shape=jax.ShapeDtypeStruct((T, D_MODEL), F32),
        compiler_params=_cparams(("parallel",)), name="final_norm")(h, moe, moe, gain)


def _bucket_table():
    n = np.arange(MAX_DIST)
    nf = np.maximum(n, 1).astype(np.float64)
    large = MAX_EXACT + (np.log(nf / MAX_EXACT) / math.log(MAX_DIST / MAX_EXACT)
                         * (N_BUCKETS - MAX_EXACT)).astype(np.int64)
    large = np.minimum(large, N_BUCKETS - 1)
    return np.where(n < MAX_EXACT, n, large).astype(np.int32)


def _band_bias(bias_d, window, npv):
    L = (npv + 1) * ATT_BLOCK
    dist = np.arange(ATT_BLOCK)[:, None] + npv * ATT_BLOCK - np.arange(L)[None, :]
    valid = (dist >= 0) & (dist < window)
    idx = np.clip(dist, 0, MAX_DIST - 1)
    tile = jnp.where(valid[None], bias_d[:, idx], NEG)
    return tile.reshape(N_KV, N_REP * ATT_BLOCK, L)


def _slc_bias(bias_d):
    tq = SEL_TQ
    dist = np.arange(tq)[:, None] + tq - np.arange(2 * tq)[None, :]
    idx = np.clip(dist, 0, MAX_DIST - 1)
    rel = bias_d[:, idx] - bias_d[:, MAX_DIST - 1][:, None, None]
    tile = jnp.where((dist >= 0)[None], rel, NEG)
    return tile.reshape(N_KV, N_REP * tq, 2 * tq)


def _prep_w_in(w):
    s = 0.125
    parts = [w[:, 0:512] * s, w[:, 768:1280] * s, w[:, 512:768], w[:, 1280:2048],
             w[:, 2072:4120], w[:, 2048:2072], jnp.zeros((D_MODEL, GATE_W - 2048 - 24), w.dtype)]
    return jnp.concatenate(parts, axis=1).astype(BF16)


def _gate_expand():
    ex = np.zeros((3, 128, 512), np.float32)
    for c in range(3):
        for h in range(N_HEADS):
            ex[c, h * 3 + c, h * HEAD_DIM:(h + 1) * HEAD_DIM] = 1.0
    return jnp.asarray(ex, BF16)


def _overlap_t(S):
    nc = (S - CMP_LEN) // CMP_STRIDE + 1
    ns = S // SEL_BLOCK
    cstart = np.arange(nc) * CMP_STRIDE
    sstart = np.arange(ns) * SEL_BLOCK
    ov = ((cstart[:, None] < sstart[None, :] + SEL_BLOCK)
          & (cstart[:, None] + CMP_LEN > sstart[None, :])).astype(np.float32)
    ovt = np.zeros((ns, S // CMP_STRIDE), np.float32)
    ovt[:, :nc] = ov.T
    return jnp.asarray(ovt, BF16)


def _block_onehot(S):
    e = (np.arange(S)[:, None] // SEL_BLOCK == np.arange(SEL_BLOCK)[None, :]).astype(np.float32)
    return jnp.asarray(e, BF16)


def kernel(x, rel_bias, norm_mix, w_in, a_sinks, cmp_pos_k, cmp_w1_k, cmp_w2_k, cmp_pos_v, cmp_w1_v,
           cmp_w2_v, w_br_a, w_br_b, w_out, norm_ffn, w_group, b_group, w_expert, b_expert, w1, w3, w2,
           norm_final):
    B, S, D = x.shape
    T = B * S
    depth = w_in.shape[0]
    assert D == D_MODEL and S % BAND_TQ == 0 and S // SEL_BLOCK <= SEL_BLOCK and T % ROW_TILE == 0

    bias_d = rel_bias[_bucket_table()].T.astype(F32)
    bias_a = _band_bias(bias_d[:N_HEADS], A_WINDOW, 1)
    bias_w = _band_bias(bias_d[N_HEADS:], B_WINDOW, 4)
    bias_s = _slc_bias(bias_d[N_HEADS:])
    ex = _gate_expand()
    ovt = _overlap_t(S)
    onehot = jnp.broadcast_to(_block_onehot(S)[None, :, None, :], (B, S, N_KV, SEL_BLOCK))
    half = CMP_STRIDE * HEAD_DIM

    h = x.reshape(T, D)
    moe = None
    for l in range(depth):
        h, qkv, gate = _inproj(h, moe, norm_mix[l].reshape(1, D), _prep_w_in(w_in[l]))

        def rows16(c):
            t = qkv[:, c * 128:(c + 1) * 128].reshape(B, S, N_KV, HEAD_DIM)
            return jnp.transpose(t, (0, 2, 1, 3)).reshape(B, N_KV, S // CMP_STRIDE, half)

        xr = jnp.stack([rows16(COL_BKC), rows16(COL_BVC)])
        pos = jnp.stack([cmp_pos_k[l].reshape(2, half), cmp_pos_v[l].reshape(2, half)])
        cw1 = jnp.stack([cmp_w1_k[l], cmp_w1_v[l]]).astype(BF16)
        cw2 = jnp.stack([cmp_w2_k[l], cmp_w2_v[l]]).astype(BF16)
        kv_cmp = _compress(xr, pos, cw1, cw2)

        o_cmp, selm = _cmpsel(qkv, kv_cmp, kv_cmp, ovt, B, S)
        ks = qkv[:, COL_BKS * 128:(COL_BKS + 1) * 128].reshape(B, S, N_KV, HEAD_DIM)
        kaug = jnp.concatenate([ks, onehot], axis=-1).reshape(T, N_KV * 128)
        o_slc = _slc(qkv, selm, kaug, bias_s, B, S)
        o_a = _banded(qkv, bias_a, a_sinks[l], 0, COL_AK, COL_AV, A_WINDOW, B, S)
        o_win = _banded(qkv, bias_w, None, 1, COL_BKW, COL_BVW, B_WINDOW, B, S)

        wr = jnp.concatenate([w_group[l], w_expert[l],
                              jnp.zeros((D, 128 - N_GROUPS - N_EXPERTS), F32)], axis=1)
        wrh = wr.astype(BF16)
        wrl = (wr - wrh.astype(F32)).astype(BF16)
        br = jnp.concatenate([b_group[l], b_expert[l],
                              jnp.zeros((128 - N_GROUPS - N_EXPERTS,), F32)]).reshape(1, 128)
        h, rinfo = _outproj(h, o_a, o_cmp, o_slc, o_win, gate, ex,
                            w_br_a[l].astype(BF16), w_br_b[l].astype(BF16), w_out[l].astype(BF16),
                            norm_ffn[l].reshape(1, D), wrh, wrl, br)

        blk_e, nused, gtok, gdst, wrow = _route(rinfo, T)
        moe = _moe(h, blk_e, nused, gtok, gdst, wrow, norm_ffn[l].reshape(1, D),
                   w1[l].astype(BF16), w3[l].astype(BF16), w2[l].astype(BF16))

    return _final(h, moe, norm_final.reshape(1, D)).reshape(B, S, D)
```

```python
import functools
import math

import numpy as np
import jax
import jax.numpy as jnp
from jax import lax
from jax.experimental import pallas as pl
from jax.experimental.pallas import tpu as pltpu

F32 = jnp.float32
BF16 = jnp.bfloat16
I32 = jnp.int32

D_MODEL = 1024
HEAD_DIM = 64
N_HEADS = 8
N_KV = 2
N_REP = 4
A_WINDOW = 128
B_WINDOW = 512
ATT_BLOCK = 128
CMP_LEN = 32
CMP_STRIDE = 16
CMP_HIDDEN = 256
SEL_BLOCK = 64
SEL_TOPN = 16
N_BUCKETS = 32
MAX_EXACT = 16
MAX_DIST = 128
N_GROUPS = 4
EXPERTS_PER_GROUP = 8
N_EXPERTS = 32
EXPERT_FF = 512
RMS_EPS = 1e-5
NEG = -1e30
FORCE = 1e9
SEL_MASK = -1e9

QKV_W = 2048
GATE_W = 2176
BG_OFF = 2048
COL_AK, COL_AV, COL_BKC, COL_BVC, COL_BKS, COL_BVS, COL_BKW, COL_BVW = 8, 9, 10, 11, 12, 13, 14, 15

ROW_TILE = 512
BAND_TQ = 512
SEL_TQ = 256
MOE_TM = 512
DMA_UNROLL = 8
VMEM_LIMIT = 56 * 1024 * 1024

_NT = (((1,), (1,)), ((), ()))


def _cparams(sem):
    return pltpu.CompilerParams(dimension_semantics=sem, vmem_limit_bytes=VMEM_LIMIT)


def _rms(h, g):
    ms = jnp.mean(h * h, axis=-1, keepdims=True)
    return (h * lax.rsqrt(ms + RMS_EPS)) * g


def _sigmoid(z):
    return 1.0 / (1.0 + jnp.exp(-z))


def _stack_heads(ref, r0, r1, c0):
    return jnp.concatenate(
        [ref[r0:r1, c0 + r * HEAD_DIM:c0 + (r + 1) * HEAD_DIM] for r in range(N_REP)], axis=0)


def _unstack_heads(o, n):
    return jnp.concatenate([o[r * n:(r + 1) * n] for r in range(N_REP)], axis=1)


def _inproj_body(with_moe, *refs):
    if with_moe:
        h_ref, m0_ref, m1_ref, g_ref, w_ref, hout_ref, qkv_ref, gate_ref = refs
        h = h_ref[...] + m0_ref[...] + m1_ref[...]
        hout_ref[...] = h
    else:
        h_ref, g_ref, w_ref, qkv_ref, gate_ref = refs
        h = h_ref[...]
    xb = _rms(h, g_ref[...]).astype(BF16)
    for c0 in range(0, QKV_W, 512):
        acc = jnp.dot(xb, w_ref[:, c0:c0 + 512], preferred_element_type=F32)
        qkv_ref[:, c0:c0 + 512] = acc.astype(BF16)
    for c0 in range(0, GATE_W, 512):
        c1 = min(c0 + 512, GATE_W)
        z = jnp.dot(xb, w_ref[:, QKV_W + c0:QKV_W + c1], preferred_element_type=F32)
        gate_ref[:, c0:c1] = _sigmoid(z).astype(BF16)


def _inproj(h, moe, gain, w_p):
    T = h.shape[0]
    tm = ROW_TILE
    nt = T // tm
    row = lambda i: (i, 0)
    const = lambda i: (0, 0)
    in_specs = [pl.BlockSpec((tm, D_MODEL), row)]
    args = [h]
    out_shape = []
    out_specs = []
    if moe is not None:
        in_specs += [pl.BlockSpec((tm, D_MODEL), row),
                     pl.BlockSpec((tm, D_MODEL), lambda i: (i + nt, 0))]
        args += [moe, moe]
        out_shape.append(jax.ShapeDtypeStruct((T, D_MODEL), F32))
        out_specs.append(pl.BlockSpec((tm, D_MODEL), row))
    in_specs += [pl.BlockSpec((1, D_MODEL), const),
                 pl.BlockSpec((D_MODEL, QKV_W + GATE_W), const)]
    args += [gain, w_p]
    out_shape += [jax.ShapeDtypeStruct((T, QKV_W), BF16), jax.ShapeDtypeStruct((T, GATE_W), BF16)]
    out_specs += [pl.BlockSpec((tm, QKV_W), row), pl.BlockSpec((tm, GATE_W), row)]
    res = pl.pallas_call(
        functools.partial(_inproj_body, moe is not None),
        grid=(nt,), in_specs=in_specs, out_specs=out_specs, out_shape=out_shape,
        compiler_params=_cparams(("parallel",)), name="inproj")(*args)
    if moe is None:
        return h, res[0], res[1]
    return res[0], res[1], res[2]


def _compress_body(x_ref, pos_ref, w1_ref, w2_ref, o_ref):
    half = CMP_STRIDE * HEAD_DIM
    for g in range(N_KV):
        x = x_ref[0, 0, g].astype(F32)
        lo = (x + pos_ref[0, 0:1, :]).astype(BF16)
        hi = (x + pos_ref[0, 1:2, :]).astype(BF16)
        a = jnp.dot(lo, w1_ref[0, 0:half, :], preferred_element_type=F32)
        b = jnp.dot(hi, w1_ref[0, half:2 * half, :], preferred_element_type=F32)
        n = b.shape[0]
        hsum = a + pltpu.roll(b, n - 1, 0)
        hid = jax.nn.gelu(hsum, approximate=True).astype(BF16)
        o_ref[0, 0, g] = jnp.dot(hid, w2_ref[0], preferred_element_type=F32).astype(BF16)


def _compress(xr, pos, w1, w2):
    _, B, G, nr, _ = xr.shape
    return pl.pallas_call(
        _compress_body,
        grid=(2, B),
        in_specs=[pl.BlockSpec((1, 1, G, nr, CMP_STRIDE * HEAD_DIM), lambda k, b: (k, b, 0, 0, 0)),
                  pl.BlockSpec((1, 2, CMP_STRIDE * HEAD_DIM), lambda k, b: (k, 0, 0)),
                  pl.BlockSpec((1, CMP_LEN * HEAD_DIM, CMP_HIDDEN), lambda k, b: (k, 0, 0)),
                  pl.BlockSpec((1, CMP_HIDDEN, HEAD_DIM), lambda k, b: (k, 0, 0))],
        out_specs=pl.BlockSpec((1, 1, G, nr, HEAD_DIM), lambda k, b: (k, b, 0, 0, 0)),
        out_shape=jax.ShapeDtypeStruct((2, B, G, nr, HEAD_DIM), BF16),
        compiler_params=_cparams(("parallel", "parallel")), name="compress")(xr, pos, w1, w2)


def _cmpsel_body(q_ref, kc_ref, vc_ref, ovt_ref, ocmp_ref, selm_ref, imp_scr):
    i = pl.program_id(1)
    tq = q_ref.shape[0]
    nc = kc_ref.shape[3]
    ns = ovt_ref.shape[0]
    t0 = i * tq
    tcol = t0 + lax.broadcasted_iota(I32, (tq, 1), 0)
    ncol = lax.broadcasted_iota(I32, (1, nc), 1)
    cval = (ncol * CMP_STRIDE + (CMP_LEN - 1)) <= tcol
    cval4 = jnp.concatenate([cval] * N_REP, axis=0)
    jrow = lax.broadcasted_iota(I32, (ns, 1), 0)
    trow = t0 + lax.broadcasted_iota(I32, (1, tq), 1)
    tb = lax.shift_right_logical(trow, 6)
    forced = (jrow == 0) | (jrow == tb) | (jrow == tb - 1)
    causal = (jrow * SEL_BLOCK) <= trow
    n_sel = float(min(SEL_TOPN, ns))
    for g in range(N_KV):
        qs = _stack_heads(q_ref, 0, tq, g * N_REP * HEAD_DIM)
        s = lax.dot_general(qs, kc_ref[0, 0, g], _NT, preferred_element_type=F32)
        sm = jnp.where(cval4, s, NEG)
        m = jnp.max(sm, axis=-1, keepdims=True)
        e = jnp.where(cval4, jnp.exp(sm - m), 0.0)
        den = jnp.sum(e, axis=-1, keepdims=True)
        pc = e / jnp.where(den > 0.0, den, 1.0)
        o = jnp.dot(pc.astype(BF16), vc_ref[0, 0, g], preferred_element_type=F32)
        ocmp_ref[:, g * 256:(g + 1) * 256] = _unstack_heads(o, tq).astype(BF16)
        pcs = pc[0:tq] + pc[tq:2 * tq] + pc[2 * tq:3 * tq] + pc[3 * tq:4 * tq]
        hi = pcs.astype(BF16)
        lo = (pcs - hi.astype(F32)).astype(BF16)
        imp = (lax.dot_general(ovt_ref[...], hi, _NT, preferred_element_type=F32)
               + lax.dot_general(ovt_ref[...], lo, _NT, preferred_element_type=F32))
        imp = jnp.where(causal, jnp.where(forced, FORCE, imp), NEG)
        imp_scr[...] = imp

        def rank_step(ii, cnt):
            rowv = imp_scr[pl.ds(ii, 1), :]
            beats = (rowv > imp) | ((rowv == imp) & (jrow > ii))
            return cnt + jnp.where(beats, 1.0, 0.0)

        cnt = lax.fori_loop(0, ns, rank_step, jnp.zeros((ns, tq), F32))
        sel = (cnt < n_sel) & (imp > NEG * 0.5)
        mt = jnp.where(sel, 0.0, SEL_MASK)
        if ns < 128:
            mt = jnp.concatenate([mt, jnp.zeros((128 - ns, tq), F32)], axis=0)
        selm_ref[:, g * 128:(g + 1) * 128] = mt.T.astype(BF16)


def _cmpsel(qkv, kcmp, vcmp, ovt, B, S):
    T = B * S
    tq = SEL_TQ
    nq = S // tq
    nc = kcmp.shape[3]
    ns = ovt.shape[0]
    return pl.pallas_call(
        _cmpsel_body,
        grid=(B, nq),
        in_specs=[pl.BlockSpec((tq, 512), lambda b, i: (b * nq + i, 1)),
                  pl.BlockSpec((1, 1, N_KV, nc, HEAD_DIM), lambda b, i: (0, b, 0, 0, 0)),
                  pl.BlockSpec((1, 1, N_KV, nc, HEAD_DIM), lambda b, i: (1, b, 0, 0, 0)),
                  pl.BlockSpec((ns, nc), lambda b, i: (0, 0))],
        out_specs=[pl.BlockSpec((tq, 512), lambda b, i: (b * nq + i, 0)),
                   pl.BlockSpec((tq, 256), lambda b, i: (b * nq + i, 0))],
        out_shape=[jax.ShapeDtypeStruct((T, 512), BF16), jax.ShapeDtypeStruct((T, 256), BF16)],
        scratch_shapes=[pltpu.VMEM((ns, tq), F32)],
        compiler_params=_cparams(("parallel", "parallel")), name="cmpsel")(qkv, kcmp, vcmp, ovt)


V_ROWS = 80


def _slc_body(q_ref, selm_ref, ka_ref, vt_ref, bias_ref, o_ref, qa_scr, m_scr, acc_scr):
    i = pl.program_id(1)
    tq = q_ref.shape[0]
    prev = jnp.maximum(i - 1, 0)

    def update(s, vt):
        m_old = m_scr[...]
        m_new = jnp.maximum(m_old, jnp.max(s, axis=0, keepdims=True))
        alpha = jnp.exp(m_old - m_new)
        p = jnp.exp(s - m_new).astype(BF16)
        acc_scr[...] = alpha * acc_scr[...] + jnp.dot(vt, p, preferred_element_type=F32)
        m_scr[...] = m_new

    def scores(c, g):
        off = pl.multiple_of(c * tq, tq)
        kt = ka_ref[pl.ds(off, tq), g * 128:(g + 1) * 128]
        return lax.dot_general(kt, qa_scr[...], _NT, preferred_element_type=F32)

    for g in range(N_KV):
        sm = selm_ref[:, g * 128:g * 128 + SEL_BLOCK]
        for r in range(N_REP):
            c0 = g * 256 + r * HEAD_DIM
            qa_scr[r * tq:(r + 1) * tq, :] = jnp.concatenate([q_ref[:, c0:c0 + HEAD_DIM], sm], axis=1)
        m_scr[...] = jnp.full(m_scr.shape, NEG, F32)
        acc_scr[...] = jnp.zeros(acc_scr.shape, F32)
        update(scores(i, g) + bias_ref[g, tq:2 * tq, :], vt_ref[0, i, g])

        @pl.when(i > 0)
        def _():
            update(scores(prev, g) + bias_ref[g, 0:tq, :], vt_ref[0, prev, g])

        def far(c, carry):
            update(scores(c, g), vt_ref[0, c, g])
            return carry

        lax.fori_loop(0, prev, far, 0)
        acc = acc_scr[...]
        ot = acc[0:HEAD_DIM] / acc[HEAD_DIM:HEAD_DIM + 1]
        pad = jnp.zeros((128 - HEAD_DIM, tq), F32)
        heads = [jnp.concatenate([ot[:, r * tq:(r + 1) * tq], pad], axis=0).T[:, 0:HEAD_DIM]
                 for r in range(N_REP)]
        o_ref[:, g * 256:(g + 1) * 256] = jnp.concatenate(heads, axis=1).astype(BF16)


def _slc(qkv, selm, kaug, vt, bias_near, B, S):
    T = B * S
    tq = SEL_TQ
    nq = S // tq
    return pl.pallas_call(
        _slc_body,
        grid=(B, nq),
        in_specs=[pl.BlockSpec((tq, 512), lambda b, i: (b * nq + i, 1)),
                  pl.BlockSpec((tq, 256), lambda b, i: (b * nq + i, 0)),
                  pl.BlockSpec((S, 256), lambda b, i: (b, 0)),
                  pl.BlockSpec((1, nq, N_KV, V_ROWS, tq), lambda b, i: (b, 0, 0, 0, 0)),
                  pl.BlockSpec((N_KV, 2 * tq, N_REP * tq), lambda b, i: (0, 0, 0))],
        out_specs=pl.BlockSpec((tq, 512), lambda b, i: (b * nq + i, 0)),
        out_shape=jax.ShapeDtypeStruct((T, 512), BF16),
        scratch_shapes=[pltpu.VMEM((N_REP * tq, 128), BF16),
                        pltpu.VMEM((1, N_REP * tq), F32),
                        pltpu.VMEM((V_ROWS, N_REP * tq), F32)],
        compiler_params=_cparams(("parallel", "arbitrary")), name="slc")(qkv, selm, kaug, vt, bias_near)


def _band_body(npv, has_sink, *refs):
    if has_sink:
        q_ref, kp_ref, km_ref, vp_ref, vm_ref, bias_ref, sink_ref, o_ref = refs
    else:
        q_ref, kp_ref, km_ref, vp_ref, vm_ref, bias_ref, o_ref = refs
    i = pl.program_id(1)
    blk = ATT_BLOCK
    L = (npv + 1) * blk
    nsub = q_ref.shape[0] // blk
    col = lax.broadcasted_iota(I32, (1, L), 1)
    for g in range(N_KV):
        ks = slice(g * HEAD_DIM, (g + 1) * HEAD_DIM)
        kfull = jnp.concatenate([kp_ref[:, ks], km_ref[:, ks]], axis=0)
        vfull = jnp.concatenate([vp_ref[:, ks], vm_ref[:, ks]], axis=0)
        if has_sink:
            sink = jnp.concatenate(
                [jnp.full((blk, 1), sink_ref[g * N_REP + r], F32) for r in range(N_REP)], axis=0)
        for sub in range(nsub):
            qs = _stack_heads(q_ref, sub * blk, (sub + 1) * blk, g * N_REP * HEAD_DIM)
            s = lax.dot_general(qs, kfull[sub * blk:sub * blk + L], _NT,
                                preferred_element_type=F32) + bias_ref[g]
            ncut = (npv - sub) * blk
            if ncut > 0:
                s = jnp.where(jnp.logical_and(col < ncut, i == 0), NEG, s)
            m = jnp.max(s, axis=-1, keepdims=True)
            if has_sink:
                m = jnp.maximum(m, sink)
            e = jnp.exp(s - m)
            den = jnp.sum(e, axis=-1, keepdims=True)
            if has_sink:
                den = den + jnp.exp(sink - m)
            o = jnp.dot(e.astype(BF16), vfull[sub * blk:sub * blk + L],
                        preferred_element_type=F32) / den
            o_ref[sub * blk:(sub + 1) * blk, g * 256:(g + 1) * 256] = _unstack_heads(o, blk).astype(BF16)


def _banded(qkv, bias, sinks, qcol, kcol, vcol, window, B, S):
    T = B * S
    tq = BAND_TQ
    nq = S // tq
    npv = -(-(window - 1) // ATT_BLOCK)
    pv = npv * ATT_BLOCK
    L = pv + ATT_BLOCK
    ratio = tq // pv
    prev_map = lambda c: (lambda b, i: (b * (S // pv) + jnp.maximum(i * ratio - 1, 0), c * (128 // 128)))
    main_map = lambda c: (lambda b, i: (b * nq + i, c))
    in_specs = [pl.BlockSpec((tq, 512), lambda b, i: (b * nq + i, qcol)),
                pl.BlockSpec((pv, 128), prev_map(kcol)), pl.BlockSpec((tq, 128), main_map(kcol)),
                pl.BlockSpec((pv, 128), prev_map(vcol)), pl.BlockSpec((tq, 128), main_map(vcol)),
                pl.BlockSpec((N_KV, N_REP * ATT_BLOCK, L), lambda b, i: (0, 0, 0))]
    args = [qkv, qkv, qkv, qkv, qkv, bias]
    if sinks is not None:
        in_specs.append(pl.BlockSpec(memory_space=pltpu.SMEM))
        args.append(sinks)
    return pl.pallas_call(
        functools.partial(_band_body, npv, sinks is not None),
        grid=(B, nq), in_specs=in_specs,
        out_specs=pl.BlockSpec((tq, 512), lambda b, i: (b * nq + i, 0)),
        out_shape=jax.ShapeDtypeStruct((T, 512), BF16),
        compiler_params=_cparams(("parallel", "parallel")),
        name="band_sink" if sinks is not None else "band_win")(*args)


def _out_body(h_ref, oa_ref, oc_ref, os_ref, ow_ref, gate_ref, ex_ref, wa_ref, wb_ref, wo_ref,
              gn_ref, wrh_ref, wrl_ref, br_ref, hout_ref, rinfo_ref):
    bgs = gate_ref[:, BG_OFF:BG_OFF + 128]
    ob = (jnp.dot(bgs, ex_ref[0], preferred_element_type=F32) * oc_ref[...].astype(F32)
          + jnp.dot(bgs, ex_ref[1], preferred_element_type=F32) * os_ref[...].astype(F32)
          + jnp.dot(bgs, ex_ref[2], preferred_element_type=F32) * ow_ref[...].astype(F32))
    ta = jnp.dot(oa_ref[...], wa_ref[...], preferred_element_type=F32)
    tb = jnp.dot(ob.astype(BF16), wb_ref[...], preferred_element_type=F32)
    merged = (gate_ref[:, 0:D_MODEL].astype(F32) * ta
              + gate_ref[:, D_MODEL:2 * D_MODEL].astype(F32) * tb)
    hn = h_ref[...] + jnp.dot(merged.astype(BF16), wo_ref[...], preferred_element_type=F32)
    hout_ref[...] = hn
    xn = _rms(hn, gn_ref[...])
    xh = xn.astype(BF16)
    xl = (xn - xh.astype(F32)).astype(BF16)
    logits = (jnp.dot(xh, wrh_ref[...], preferred_element_type=F32)
              + jnp.dot(xl, wrh_ref[...], preferred_element_type=F32)
              + jnp.dot(xh, wrl_ref[...], preferred_element_type=F32)) + br_ref[...]
    tm = logits.shape[0]
    lane = lax.broadcasted_iota(I32, (1, 128), 1).astype(F32)
    big = 1e9
    is_g = lane < N_GROUPS
    glog = jnp.where(is_g, logits, NEG)
    gmax = jnp.max(glog, axis=-1, keepdims=True)
    gsel = jnp.min(jnp.where(glog == gmax, lane, big), axis=-1, keepdims=True)
    gsum = jnp.sum(jnp.where(is_g, jnp.exp(logits - gmax), 0.0), axis=-1, keepdims=True)
    gw = 1.0 / gsum
    e_lo = N_GROUPS + gsel * EXPERTS_PER_GROUP
    in_g = (lane >= e_lo) & (lane < e_lo + EXPERTS_PER_GROUP)
    ev = jnp.where(in_g, logits, NEG)
    v1 = jnp.max(ev, axis=-1, keepdims=True)
    i1 = jnp.min(jnp.where(ev == v1, lane, big), axis=-1, keepdims=True)
    ev2 = jnp.where(lane == i1, NEG, ev)
    v2 = jnp.max(ev2, axis=-1, keepdims=True)
    i2 = jnp.min(jnp.where(ev2 == v2, lane, big), axis=-1, keepdims=True)
    d = jnp.exp(v2 - v1)
    p1 = 1.0 / (1.0 + d)
    p2 = d / (1.0 + d)
    lane8 = lax.broadcasted_iota(I32, (tm, 8), 1)
    rinfo_ref[...] = jnp.where(lane8 == 0, i1 - N_GROUPS,
                     jnp.where(lane8 == 1, i2 - N_GROUPS,
                     jnp.where(lane8 == 2, p1 * gw,
                     jnp.where(lane8 == 3, p2 * gw, 0.0))))


def _outproj(h, oa, oc, osl, ow, gate, ex, wa, wb, wo, gn, wrh, wrl, br):
    T = h.shape[0]
    tm = ROW_TILE
    row = lambda i: (i, 0)
    c2 = lambda i: (0, 0)
    c3 = lambda i: (0, 0, 0)
    return pl.pallas_call(
        _out_body,
        grid=(T // tm,),
        in_specs=[pl.BlockSpec((tm, D_MODEL), row),
                  pl.BlockSpec((tm, 512), row), pl.BlockSpec((tm, 512), row),
                  pl.BlockSpec((tm, 512), row), pl.BlockSpec((tm, 512), row),
                  pl.BlockSpec((tm, GATE_W), row),
                  pl.BlockSpec((3, 128, 512), c3),
                  pl.BlockSpec((512, D_MODEL), c2), pl.BlockSpec((512, D_MODEL), c2),
                  pl.BlockSpec((D_MODEL, D_MODEL), c2),
                  pl.BlockSpec((1, D_MODEL), c2),
                  pl.BlockSpec((D_MODEL, 128), c2), pl.BlockSpec((D_MODEL, 128), c2),
                  pl.BlockSpec((1, 128), c2)],
        out_specs=[pl.BlockSpec((tm, D_MODEL), row), pl.BlockSpec((tm, 8), row)],
        out_shape=[jax.ShapeDtypeStruct((T, D_MODEL), F32), jax.ShapeDtypeStruct((T, 8), F32)],
        compiler_params=_cparams(("parallel",)), name="outproj")(
            h, oa, oc, osl, ow, gate, ex, wa, wb, wo, gn, wrh, wrl, br)


def _moe_body(be_ref, nu_ref, gtok_ref, gtokn_ref, gdst_ref, wrow_ref, gn_ref, w1_ref, w3_ref, w2_ref,
              h_hbm, out_hbm, xbuf, ybuf, gsem, ssem):
    i = pl.program_id(0)
    nu = nu_ref[0]
    tm = xbuf.shape[1]
    slot = lax.rem(i, 2)

    def gather_copy(idx_ref, sl, r):
        t = idx_ref[0, 0, r]
        return pltpu.make_async_copy(h_hbm.at[pl.ds(t, 1)], xbuf.at[sl, pl.ds(r, 1)], gsem.at[sl])

    def scatter_copy(r):
        d = gdst_ref[0, 0, r]
        return pltpu.make_async_copy(ybuf.at[pl.ds(r, 1)], out_hbm.at[pl.ds(d, 1)], ssem.at[0])

    def start_gather(idx_ref, sl):
        def body(r, c):
            gather_copy(idx_ref, sl, r).start()
            return c
        lax.fori_loop(0, tm, body, 0, unroll=DMA_UNROLL)

    @pl.when(i == 0)
    def _():
        start_gather(gtok_ref, 0)
        ybuf[...] = jnp.zeros(ybuf.shape, F32)
        fill = pltpu.make_async_copy(ybuf, out_hbm.at[pl.ds(out_hbm.shape[0] - tm, tm)], ssem.at[0])
        fill.start()
        fill.wait()

    @pl.when(i + 1 < nu)
    def _():
        start_gather(gtokn_ref, 1 - slot)

    @pl.when(i < nu)
    def _():
        def wbody(r, c):
            gather_copy(gtok_ref, slot, r).wait()
            return c
        lax.fori_loop(0, tm, wbody, 0, unroll=DMA_UNROLL)
        x = _rms(xbuf[slot], gn_ref[...]).astype(BF16)
        h1 = jnp.dot(x, w1_ref[0], preferred_element_type=F32)
        h3 = jnp.dot(x, w3_ref[0], preferred_element_type=F32)
        act = (h1 * _sigmoid(h1) * h3).astype(BF16)
        y = jnp.dot(act, w2_ref[0], preferred_element_type=F32)
        ybuf[...] = y * wrow_ref[...]

        def sbody(r, c):
            scatter_copy(r).start()
            return c
        lax.fori_loop(0, tm, sbody, 0, unroll=DMA_UNROLL)

        def swbody(r, c):
            scatter_copy(r).wait()
            return c
        lax.fori_loop(0, tm, swbody, 0, unroll=DMA_UNROLL)


def _moe(h, blk_e, nused, gtok, gdst, wrow, gn, w1, w3, w2):
    T = h.shape[0]
    tm = MOE_TM
    nblk = gtok.shape[0]
    grid_spec = pltpu.PrefetchScalarGridSpec(
        num_scalar_prefetch=2,
        grid=(nblk,),
        in_specs=[pl.BlockSpec((1, 1, tm), lambda i, be, nu: (i, 0, 0), memory_space=pltpu.SMEM),
                  pl.BlockSpec((1, 1, tm), lambda i, be, nu: (jnp.minimum(i + 1, nblk - 1), 0, 0),
                               memory_space=pltpu.SMEM),
                  pl.BlockSpec((1, 1, tm), lambda i, be, nu: (i, 0, 0), memory_space=pltpu.SMEM),
                  pl.BlockSpec((tm, 1), lambda i, be, nu: (i, 0)),
                  pl.BlockSpec((1, D_MODEL), lambda i, be, nu: (0, 0)),
                  pl.BlockSpec((1, D_MODEL, EXPERT_FF), lambda i, be, nu: (be[i], 0, 0)),
                  pl.BlockSpec((1, D_MODEL, EXPERT_FF), lambda i, be, nu: (be[i], 0, 0)),
                  pl.BlockSpec((1, EXPERT_FF, D_MODEL), lambda i, be, nu: (be[i], 0, 0)),
                  pl.BlockSpec(memory_space=pl.ANY)],
        out_specs=pl.BlockSpec(memory_space=pl.ANY),
        scratch_shapes=[pltpu.VMEM((2, tm, D_MODEL), F32),
                        pltpu.VMEM((tm, D_MODEL), F32),
                        pltpu.SemaphoreType.DMA((2,)),
                        pltpu.SemaphoreType.DMA((1,))])
    return pl.pallas_call(
        _moe_body, grid_spec=grid_spec,
        out_shape=jax.ShapeDtypeStruct((2 * T + tm, D_MODEL), F32),
        compiler_params=_cparams(("arbitrary",)), name="moe")(
            blk_e, nused, gtok, gtok, gdst, wrow, gn, w1, w3, w2, h)


def _route(rinfo, T):
    tm = MOE_TM
    A = 2 * T
    e_flat = rinfo[:, 0:2].astype(I32).T.reshape(A)
    w_flat = rinfo[:, 2:4].T.reshape(A)
    onehot = (e_flat[:, None] == jnp.arange(N_EXPERTS, dtype=I32)[None, :]).astype(I32)
    csum = jnp.cumsum(onehot, axis=0)
    counts = csum[-1]
    padded = (counts + tm - 1) // tm * tm
    pend = jnp.cumsum(padded)
    pstart = pend - padded
    dest = jnp.sum(onehot * (csum - 1 + pstart[None, :]), axis=1)
    P = A + N_EXPERTS * tm
    nblk = P // tm
    a = jnp.arange(A, dtype=I32)
    tok = jnp.where(a >= T, a - T, a)
    upd = jnp.stack([tok, a, lax.bitcast_convert_type(w_flat, I32)], axis=1)
    base = jnp.stack([jnp.zeros((P,), I32), A + jnp.arange(P, dtype=I32) % tm, jnp.zeros((P,), I32)], axis=1)
    packed = base.at[dest].set(upd)
    gtok, gdst = packed[:, 0], packed[:, 1]
    wrow = lax.bitcast_convert_type(packed[:, 2], F32)
    blk_e = jnp.clip(jnp.searchsorted(pend, jnp.arange(nblk, dtype=I32) * tm, side='right'),
                     0, N_EXPERTS - 1).astype(I32)
    nused = (pend[-1] // tm).astype(I32).reshape(1)
    return blk_e, nused, gtok.reshape(nblk, 1, tm), gdst.reshape(nblk, 1, tm), wrow.reshape(P, 1)


def _final_body(h_ref, m0_ref, m1_ref, g_ref, o_ref):
    o_ref[...] = _rms(h_ref[...] + m0_ref[...] + m1_ref[...], g_ref[...])


def _final(h, moe, gain):
    T = h.shape[0]
    tm = ROW_TILE
    nt = T // tm
    row = lambda i: (i, 0)
    return pl.pallas_call(
        _final_body, grid=(nt,),
        in_specs=[pl.BlockSpec((tm, D_MODEL), row), pl.BlockSpec((tm, D_MODEL), row),
                  pl.BlockSpec((tm, D_MODEL), lambda i: (i + nt, 0)),
                  pl.BlockSpec((1, D_MODEL), lambda i: (0, 0))],
        out_specs=pl.BlockSpec((tm, D_MODEL), row),
        out_shape=jax.ShapeDtypeStruct((T, D_MODEL), F32),
        compiler_params=_cparams(("parallel",)), name="final_norm")(h, moe, moe, gain)


def _bucket_table():
    n = np.arange(MAX_DIST)
    nf = np.maximum(n, 1).astype(np.float64)
    large = MAX_EXACT + (np.log(nf / MAX_EXACT) / math.log(MAX_DIST / MAX_EXACT)
                         * (N_BUCKETS - MAX_EXACT)).astype(np.int64)
    large = np.minimum(large, N_BUCKETS - 1)
    return np.where(n < MAX_EXACT, n, large).astype(np.int32)


def _band_bias(bias_d, window, npv):
    L = (npv + 1) * ATT_BLOCK
    dist = np.arange(ATT_BLOCK)[:, None] + npv * ATT_BLOCK - np.arange(L)[None, :]
    valid = (dist >= 0) & (dist < window)
    idx = np.clip(dist, 0, MAX_DIST - 1)
    tile = jnp.where(valid[None], bias_d[:, idx], NEG)
    return tile.reshape(N_KV, N_REP * ATT_BLOCK, L)


def _slc_bias(bias_d):
    tq = SEL_TQ
    dist = np.arange(tq)[:, None] + tq - np.arange(2 * tq)[None, :]
    idx = np.clip(dist, 0, MAX_DIST - 1)
    rel = bias_d[:, idx] - bias_d[:, MAX_DIST - 1][:, None, None]
    tile = jnp.where((dist >= 0)[None], rel, NEG)
    return jnp.transpose(tile.reshape(N_KV, N_REP * tq, 2 * tq), (0, 2, 1))


def _prep_w_in(w):
    s = 0.125
    parts = [w[:, 0:512] * s, w[:, 768:1280] * s, w[:, 512:768], w[:, 1280:2048],
             w[:, 2072:4120], w[:, 2048:2072], jnp.zeros((D_MODEL, GATE_W - 2048 - 24), w.dtype)]
    return jnp.concatenate(parts, axis=1).astype(BF16)


def _gate_expand():
    ex = np.zeros((3, 128, 512), np.float32)
    for c in range(3):
        for h in range(N_HEADS):
            ex[c, h * 3 + c, h * HEAD_DIM:(h + 1) * HEAD_DIM] = 1.0
    return jnp.asarray(ex, BF16)


def _overlap_t(S):
    nc = (S - CMP_LEN) // CMP_STRIDE + 1
    ns = S // SEL_BLOCK
    cstart = np.arange(nc) * CMP_STRIDE
    sstart = np.arange(ns) * SEL_BLOCK
    ov = ((cstart[:, None] < sstart[None, :] + SEL_BLOCK)
          & (cstart[:, None] + CMP_LEN > sstart[None, :])).astype(np.float32)
    ovt = np.zeros((ns, S // CMP_STRIDE), np.float32)
    ovt[:, :nc] = ov.T
    return jnp.asarray(ovt, BF16)


def _block_onehot(S):
    e = (np.arange(S)[:, None] // SEL_BLOCK == np.arange(SEL_BLOCK)[None, :]).astype(np.float32)
    return jnp.asarray(e, BF16)


def kernel(x, rel_bias, norm_mix, w_in, a_sinks, cmp_pos_k, cmp_w1_k, cmp_w2_k, cmp_pos_v, cmp_w1_v,
           cmp_w2_v, w_br_a, w_br_b, w_out, norm_ffn, w_group, b_group, w_expert, b_expert, w1, w3, w2,
           norm_final):
    B, S, D = x.shape
    T = B * S
    depth = w_in.shape[0]
    assert D == D_MODEL and S % BAND_TQ == 0 and S // SEL_BLOCK <= SEL_BLOCK and T % ROW_TILE == 0

    bias_d = rel_bias[_bucket_table()].T.astype(F32)
    bias_a = _band_bias(bias_d[:N_HEADS], A_WINDOW, 1)
    bias_w = _band_bias(bias_d[N_HEADS:], B_WINDOW, 4)
    bias_s = _slc_bias(bias_d[N_HEADS:])
    ex = _gate_expand()
    ovt = _overlap_t(S)
    onehot = jnp.broadcast_to(_block_onehot(S)[None, :, None, :], (B, S, N_KV, SEL_BLOCK))
    half = CMP_STRIDE * HEAD_DIM

    h = x.reshape(T, D)
    moe = None
    for l in range(depth):
        h, qkv, gate = _inproj(h, moe, norm_mix[l].reshape(1, D), _prep_w_in(w_in[l]))

        def rows16(c):
            t = qkv[:, c * 128:(c + 1) * 128].reshape(B, S, N_KV, HEAD_DIM)
            return jnp.transpose(t, (0, 2, 1, 3)).reshape(B, N_KV, S // CMP_STRIDE, half)

        xr = jnp.stack([rows16(COL_BKC), rows16(COL_BVC)])
        pos = jnp.stack([cmp_pos_k[l].reshape(2, half), cmp_pos_v[l].reshape(2, half)])
        cw1 = jnp.stack([cmp_w1_k[l], cmp_w1_v[l]]).astype(BF16)
        cw2 = jnp.stack([cmp_w2_k[l], cmp_w2_v[l]]).astype(BF16)
        kv_cmp = _compress(xr, pos, cw1, cw2)

        o_cmp, selm = _cmpsel(qkv, kv_cmp, kv_cmp, ovt, B, S)
        ks = qkv[:, COL_BKS * 128:(COL_BKS + 1) * 128].reshape(B, S, N_KV, HEAD_DIM)
        kaug = jnp.concatenate([ks, onehot], axis=-1).reshape(T, N_KV * 128)
        nch = S // SEL_TQ
        vs = qkv[:, COL_BVS * 128:(COL_BVS + 1) * 128].reshape(B, nch, SEL_TQ, N_KV, HEAD_DIM)
        vt = jnp.concatenate([jnp.transpose(vs, (0, 1, 3, 4, 2)),
                              jnp.ones((B, nch, N_KV, 1, SEL_TQ), BF16),
                              jnp.zeros((B, nch, N_KV, V_ROWS - HEAD_DIM - 1, SEL_TQ), BF16)], axis=3)
        o_slc = _slc(qkv, selm, kaug, vt, bias_s, B, S)
        o_a = _banded(qkv, bias_a, a_sinks[l], 0, COL_AK, COL_AV, A_WINDOW, B, S)
        o_win = _banded(qkv, bias_w, None, 1, COL_BKW, COL_BVW, B_WINDOW, B, S)

        wr = jnp.concatenate([w_group[l], w_expert[l],
                              jnp.zeros((D, 128 - N_GROUPS - N_EXPERTS), F32)], axis=1)
        wrh = wr.astype(BF16)
        wrl = (wr - wrh.astype(F32)).astype(BF16)
        br = jnp.concatenate([b_group[l], b_expert[l],
                              jnp.zeros((128 - N_GROUPS - N_EXPERTS,), F32)]).reshape(1, 128)
        h, rinfo = _outproj(h, o_a, o_cmp, o_slc, o_win, gate, ex,
                            w_br_a[l].astype(BF16), w_br_b[l].astype(BF16), w_out[l].astype(BF16),
                            norm_ffn[l].reshape(1, D), wrh, wrl, br)

        blk_e, nused, gtok, gdst, wrow = _route(rinfo, T)
        moe = _moe(h, blk_e, nused, gtok, gdst, wrow, norm_ffn[l].reshape(1, D),
                   w1[l].astype(BF16), w3[l].astype(BF16), w2[l].astype(BF16))

    return _final(h, moe, norm_final.reshape(1, D)).reshape(B, S, D)
```

```python
import functools
import math

import numpy as np
import jax
import jax.numpy as jnp
from jax import lax
from jax.experimental import pallas as pl
from jax.experimental.pallas import tpu as pltpu

F32 = jnp.float32
BF16 = jnp.bfloat16
I32 = jnp.int32

D_MODEL = 1024
HEAD_DIM = 64
N_HEADS = 8
N_KV = 2
N_REP = 4
A_WINDOW = 128
B_WINDOW = 512
ATT_BLOCK = 128
CMP_LEN = 32
CMP_STRIDE = 16
CMP_HIDDEN = 256
SEL_BLOCK = 64
SEL_TOPN = 16
N_BUCKETS = 32
MAX_EXACT = 16
MAX_DIST = 128
N_GROUPS = 4
EXPERTS_PER_GROUP = 8
N_EXPERTS = 32
EXPERT_FF = 512
RMS_EPS = 1e-5
NEG = -1e30
FORCE = 1e9
SEL_MASK = -1e9

QKV_W = 2048
GATE_W = 2176
BG_OFF = 2048
COL_AK, COL_AV, COL_BKC, COL_BVC, COL_BKS, COL_BVS, COL_BKW, COL_BVW = 8, 9, 10, 11, 12, 13, 14, 15

ROW_TILE = 512
BAND_TQ = 512
SEL_TQ = 256
MOE_TM = 512
DMA_UNROLL = 16
ROW_CHUNKS = D_MODEL // 128
VMEM_LIMIT = 56 * 1024 * 1024

_NT = (((1,), (1,)), ((), ()))


def _cparams(sem):
    return pltpu.CompilerParams(dimension_semantics=sem, vmem_limit_bytes=VMEM_LIMIT)


def _rms(h, g):
    ms = jnp.mean(h * h, axis=-1, keepdims=True)
    return (h * lax.rsqrt(ms + RMS_EPS)) * g


def _sigmoid(z):
    return 1.0 / (1.0 + jnp.exp(-z))


def _stack_heads(ref, r0, r1, c0):
    return jnp.concatenate(
        [ref[r0:r1, c0 + r * HEAD_DIM:c0 + (r + 1) * HEAD_DIM] for r in range(N_REP)], axis=0)


def _rows_from_tiles(*refs):
    n = refs[0].shape[0] // ROW_CHUNKS
    return jnp.concatenate(
        [sum(r[pl.ds(c, n, stride=ROW_CHUNKS), :] for r in refs) for c in range(ROW_CHUNKS)], axis=1)


def _rows_to_tiles(ref, x):
    n = x.shape[0]
    for c in range(ROW_CHUNKS):
        ref[pl.ds(c, n, stride=ROW_CHUNKS), :] = x[:, c * 128:(c + 1) * 128]


def _unstack_heads(o, n):
    return jnp.concatenate([o[r * n:(r + 1) * n] for r in range(N_REP)], axis=1)


def _inproj_body(with_moe, *refs):
    if with_moe:
        h_ref, m0_ref, m1_ref, g_ref, w_ref, hout_ref, qkv_ref, gate_ref = refs
        h = _rows_from_tiles(h_ref, m0_ref, m1_ref)
        hout_ref[...] = h
    else:
        h_ref, g_ref, w_ref, qkv_ref, gate_ref = refs
        h = h_ref[...]
    xb = _rms(h, g_ref[...]).astype(BF16)
    for c0 in range(0, QKV_W, 512):
        acc = jnp.dot(xb, w_ref[:, c0:c0 + 512], preferred_element_type=F32)
        qkv_ref[:, c0:c0 + 512] = acc.astype(BF16)
    for c0 in range(0, GATE_W, 512):
        c1 = min(c0 + 512, GATE_W)
        z = jnp.dot(xb, w_ref[:, QKV_W + c0:QKV_W + c1], preferred_element_type=F32)
        gate_ref[:, c0:c1] = _sigmoid(z).astype(BF16)


def _inproj(h, moe, gain, w_p):
    T = h.shape[0] if moe is None else h.shape[0] // ROW_CHUNKS
    tm = ROW_TILE
    nt = T // tm
    row = lambda i: (i, 0)
    const = lambda i: (0, 0)
    tile = (tm * ROW_CHUNKS, 128)
    in_specs = [pl.BlockSpec((tm, D_MODEL), row) if moe is None else pl.BlockSpec(tile, row)]
    args = [h]
    out_shape = []
    out_specs = []
    if moe is not None:
        in_specs += [pl.BlockSpec(tile, row), pl.BlockSpec(tile, lambda i: (i + nt, 0))]
        args += [moe, moe]
        out_shape.append(jax.ShapeDtypeStruct((T, D_MODEL), F32))
        out_specs.append(pl.BlockSpec((tm, D_MODEL), row))
    in_specs += [pl.BlockSpec((1, D_MODEL), const),
                 pl.BlockSpec((D_MODEL, QKV_W + GATE_W), const)]
    args += [gain, w_p]
    out_shape += [jax.ShapeDtypeStruct((T, QKV_W), BF16), jax.ShapeDtypeStruct((T, GATE_W), BF16)]
    out_specs += [pl.BlockSpec((tm, QKV_W), row), pl.BlockSpec((tm, GATE_W), row)]
    res = pl.pallas_call(
        functools.partial(_inproj_body, moe is not None),
        grid=(nt,), in_specs=in_specs, out_specs=out_specs, out_shape=out_shape,
        compiler_params=_cparams(("parallel",)), name="inproj")(*args)
    if moe is None:
        return h, res[0], res[1]
    return res[0], res[1], res[2]


def _compress_body(x_ref, pos_ref, w1_ref, w2_ref, o_ref):
    half = CMP_STRIDE * HEAD_DIM
    for g in range(N_KV):
        x = x_ref[0, 0, g].astype(F32)
        lo = (x + pos_ref[0, 0:1, :]).astype(BF16)
        hi = (x + pos_ref[0, 1:2, :]).astype(BF16)
        a = jnp.dot(lo, w1_ref[0, 0:half, :], preferred_element_type=F32)
        b = jnp.dot(hi, w1_ref[0, half:2 * half, :], preferred_element_type=F32)
        n = b.shape[0]
        hsum = a + pltpu.roll(b, n - 1, 0)
        hid = jax.nn.gelu(hsum, approximate=True).astype(BF16)
        o_ref[0, 0, g] = jnp.dot(hid, w2_ref[0], preferred_element_type=F32).astype(BF16)


def _compress(xr, pos, w1, w2):
    _, B, G, nr, _ = xr.shape
    return pl.pallas_call(
        _compress_body,
        grid=(2, B),
        in_specs=[pl.BlockSpec((1, 1, G, nr, CMP_STRIDE * HEAD_DIM), lambda k, b: (k, b, 0, 0, 0)),
                  pl.BlockSpec((1, 2, CMP_STRIDE * HEAD_DIM), lambda k, b: (k, 0, 0)),
                  pl.BlockSpec((1, CMP_LEN * HEAD_DIM, CMP_HIDDEN), lambda k, b: (k, 0, 0)),
                  pl.BlockSpec((1, CMP_HIDDEN, HEAD_DIM), lambda k, b: (k, 0, 0))],
        out_specs=pl.BlockSpec((1, 1, G, nr, HEAD_DIM), lambda k, b: (k, b, 0, 0, 0)),
        out_shape=jax.ShapeDtypeStruct((2, B, G, nr, HEAD_DIM), BF16),
        compiler_params=_cparams(("parallel", "parallel")), name="compress")(xr, pos, w1, w2)


def _cmpsel_body(q_ref, kc_ref, vc_ref, ovt_ref, ocmp_ref, selm_ref, imp_scr):
    i = pl.program_id(1)
    tq = q_ref.shape[0]
    nc = kc_ref.shape[3]
    ns = ovt_ref.shape[0]
    t0 = i * tq
    tcol = t0 + lax.broadcasted_iota(I32, (tq, 1), 0)
    ncol = lax.broadcasted_iota(I32, (1, nc), 1)
    cval = (ncol * CMP_STRIDE + (CMP_LEN - 1)) <= tcol
    cval4 = jnp.concatenate([cval] * N_REP, axis=0)
    jrow = lax.broadcasted_iota(I32, (ns, 1), 0)
    trow = t0 + lax.broadcasted_iota(I32, (1, tq), 1)
    tb = lax.shift_right_logical(trow, 6)
    forced = (jrow == 0) | (jrow == tb) | (jrow == tb - 1)
    causal = (jrow * SEL_BLOCK) <= trow
    n_sel = float(min(SEL_TOPN, ns))
    for g in range(N_KV):
        qs = _stack_heads(q_ref, 0, tq, g * N_REP * HEAD_DIM)
        s = lax.dot_general(qs, kc_ref[0, 0, g], _NT, preferred_element_type=F32)
        sm = jnp.where(cval4, s, NEG)
        m = jnp.max(sm, axis=-1, keepdims=True)
        e = jnp.where(cval4, jnp.exp(sm - m), 0.0)
        den = jnp.sum(e, axis=-1, keepdims=True)
        pc = e / jnp.where(den > 0.0, den, 1.0)
        o = jnp.dot(pc.astype(BF16), vc_ref[0, 0, g], preferred_element_type=F32)
        ocmp_ref[:, g * 256:(g + 1) * 256] = _unstack_heads(o, tq).astype(BF16)
        pcs = pc[0:tq] + pc[tq:2 * tq] + pc[2 * tq:3 * tq] + pc[3 * tq:4 * tq]
        hi = pcs.astype(BF16)
        lo = (pcs - hi.astype(F32)).astype(BF16)
        imp = (lax.dot_general(ovt_ref[...], hi, _NT, preferred_element_type=F32)
               + lax.dot_general(ovt_ref[...], lo, _NT, preferred_element_type=F32))
        imp = jnp.where(causal, jnp.where(forced, FORCE, imp), NEG)
        imp_scr[...] = imp

        def rank_step(ii, cnt):
            rowv = imp_scr[pl.ds(ii, 1), :]
            beats = (rowv > imp) | ((rowv == imp) & (jrow > ii))
            return cnt + jnp.where(beats, 1.0, 0.0)

        cnt = lax.fori_loop(0, ns, rank_step, jnp.zeros((ns, tq), F32))
        sel = (cnt < n_sel) & (imp > NEG * 0.5)
        mt = jnp.where(sel, 0.0, SEL_MASK)
        if ns < 128:
            mt = jnp.concatenate([mt, jnp.zeros((128 - ns, tq), F32)], axis=0)
        selm_ref[:, g * 128:(g + 1) * 128] = mt.T.astype(BF16)


def _cmpsel(qkv, kcmp, vcmp, ovt, B, S):
    T = B * S
    tq = SEL_TQ
    nq = S // tq
    nc = kcmp.shape[3]
    ns = ovt.shape[0]
    return pl.pallas_call(
        _cmpsel_body,
        grid=(B, nq),
        in_specs=[pl.BlockSpec((tq, 512), lambda b, i: (b * nq + i, 1)),
                  pl.BlockSpec((1, 1, N_KV, nc, HEAD_DIM), lambda b, i: (0, b, 0, 0, 0)),
                  pl.BlockSpec((1, 1, N_KV, nc, HEAD_DIM), lambda b, i: (1, b, 0, 0, 0)),
                  pl.BlockSpec((ns, nc), lambda b, i: (0, 0))],
        out_specs=[pl.BlockSpec((tq, 512), lambda b, i: (b * nq + i, 0)),
                   pl.BlockSpec((tq, 256), lambda b, i: (b * nq + i, 0))],
        out_shape=[jax.ShapeDtypeStruct((T, 512), BF16), jax.ShapeDtypeStruct((T, 256), BF16)],
        scratch_shapes=[pltpu.VMEM((ns, tq), F32)],
        compiler_params=_cparams(("parallel", "parallel")), name="cmpsel")(qkv, kcmp, vcmp, ovt)


V_ROWS = 80


def _slc_body(q_ref, selm_ref, ka_ref, vt_ref, bias_ref, o_ref, qa_scr, m_scr, acc_scr):
    i = pl.program_id(1)
    tq = q_ref.shape[0]
    prev = jnp.maximum(i - 1, 0)

    def update(s, vt):
        m_old = m_scr[...]
        m_new = jnp.maximum(m_old, jnp.max(s, axis=0, keepdims=True))
        alpha = jnp.exp(m_old - m_new)
        p = jnp.exp(s - m_new).astype(BF16)
        acc_scr[...] = alpha * acc_scr[...] + jnp.dot(vt, p, preferred_element_type=F32)
        m_scr[...] = m_new

    def scores(c, g):
        off = pl.multiple_of(c * tq, tq)
        kt = ka_ref[pl.ds(off, tq), g * 128:(g + 1) * 128]
        return lax.dot_general(kt, qa_scr[...], _NT, preferred_element_type=F32)

    for g in range(N_KV):
        sm = selm_ref[:, g * 128:g * 128 + SEL_BLOCK]
        for r in range(N_REP):
            c0 = g * 256 + r * HEAD_DIM
            qa_scr[r * tq:(r + 1) * tq, :] = jnp.concatenate([q_ref[:, c0:c0 + HEAD_DIM], sm], axis=1)
        m_scr[...] = jnp.full(m_scr.shape, NEG, F32)
        acc_scr[...] = jnp.zeros(acc_scr.shape, F32)
        update(scores(i, g) + bias_ref[g, tq:2 * tq, :], vt_ref[0, i, g])

        @pl.when(i > 0)
        def _():
            update(scores(prev, g) + bias_ref[g, 0:tq, :], vt_ref[0, prev, g])

        def far(c2, carry):
            c = 2 * c2
            off = pl.multiple_of(c * tq, 2 * tq)
            kt = ka_ref[pl.ds(off, 2 * tq), g * 128:(g + 1) * 128]
            s = lax.dot_general(kt, qa_scr[...], _NT, preferred_element_type=F32)
            update(s, jnp.concatenate([vt_ref[0, c, g], vt_ref[0, c + 1, g]], axis=1))
            return carry

        lax.fori_loop(0, lax.shift_right_logical(prev, 1), far, 0)

        @pl.when(lax.rem(prev, 2) == 1)
        def _():
            update(scores(prev - 1, g), vt_ref[0, prev - 1, g])
        acc = acc_scr[...]
        ot = acc[0:HEAD_DIM] / acc[HEAD_DIM:HEAD_DIM + 1]
        pad = jnp.zeros((128 - HEAD_DIM, tq), F32)
        heads = [jnp.concatenate([ot[:, r * tq:(r + 1) * tq], pad], axis=0).T[:, 0:HEAD_DIM]
                 for r in range(N_REP)]
        o_ref[:, g * 256:(g + 1) * 256] = jnp.concatenate(heads, axis=1).astype(BF16)


def _slc(qkv, selm, kaug, vt, bias_near, B, S):
    T = B * S
    tq = SEL_TQ
    nq = S // tq
    return pl.pallas_call(
        _slc_body,
        grid=(B, nq),
        in_specs=[pl.BlockSpec((tq, 512), lambda b, i: (b * nq + i, 1)),
                  pl.BlockSpec((tq, 256), lambda b, i: (b * nq + i, 0)),
                  pl.BlockSpec((S, 256), lambda b, i: (b, 0)),
                  pl.BlockSpec((1, nq, N_KV, V_ROWS, tq), lambda b, i: (b, 0, 0, 0, 0)),
                  pl.BlockSpec((N_KV, 2 * tq, N_REP * tq), lambda b, i: (0, 0, 0))],
        out_specs=pl.BlockSpec((tq, 512), lambda b, i: (b * nq + i, 0)),
        out_shape=jax.ShapeDtypeStruct((T, 512), BF16),
        scratch_shapes=[pltpu.VMEM((N_REP * tq, 128), BF16),
                        pltpu.VMEM((1, N_REP * tq), F32),
                        pltpu.VMEM((V_ROWS, N_REP * tq), F32)],
        compiler_params=_cparams(("parallel", "arbitrary")), name="slc")(qkv, selm, kaug, vt, bias_near)


def _band_body(npv, has_sink, *refs):
    if has_sink:
        q_ref, kp_ref, km_ref, vp_ref, vm_ref, bias_ref, sink_ref, o_ref = refs
    else:
        q_ref, kp_ref, km_ref, vp_ref, vm_ref, bias_ref, o_ref = refs
    i = pl.program_id(1)
    blk = ATT_BLOCK
    L = (npv + 1) * blk
    nsub = q_ref.shape[0] // blk
    col = lax.broadcasted_iota(I32, (1, L), 1)
    for g in range(N_KV):
        ks = slice(g * HEAD_DIM, (g + 1) * HEAD_DIM)
        kfull = jnp.concatenate([kp_ref[:, ks], km_ref[:, ks]], axis=0)
        vfull = jnp.concatenate([vp_ref[:, ks], vm_ref[:, ks]], axis=0)
        if has_sink:
            sink = jnp.concatenate(
                [jnp.full((blk, 1), sink_ref[g * N_REP + r], F32) for r in range(N_REP)], axis=0)
        for sub in range(nsub):
            qs = _stack_heads(q_ref, sub * blk, (sub + 1) * blk, g * N_REP * HEAD_DIM)
            s = lax.dot_general(qs, kfull[sub * blk:sub * blk + L], _NT,
                                preferred_element_type=F32) + bias_ref[g]
            ncut = (npv - sub) * blk
            if ncut > 0:
                s = jnp.where(jnp.logical_and(col < ncut, i == 0), NEG, s)
            m = jnp.max(s, axis=-1, keepdims=True)
            if has_sink:
                m = jnp.maximum(m, sink)
            e = jnp.exp(s - m)
            den = jnp.sum(e, axis=-1, keepdims=True)
            if has_sink:
                den = den + jnp.exp(sink - m)
            o = jnp.dot(e.astype(BF16), vfull[sub * blk:sub * blk + L],
                        preferred_element_type=F32) / den
            o_ref[sub * blk:(sub + 1) * blk, g * 256:(g + 1) * 256] = _unstack_heads(o, blk).astype(BF16)


def _banded(qkv, bias, sinks, qcol, kcol, vcol, window, B, S):
    T = B * S
    tq = BAND_TQ
    nq = S // tq
    npv = -(-(window - 1) // ATT_BLOCK)
    pv = npv * ATT_BLOCK
    L = pv + ATT_BLOCK
    ratio = tq // pv
    prev_map = lambda c: (lambda b, i: (b * (S // pv) + jnp.maximum(i * ratio - 1, 0), c * (128 // 128)))
    main_map = lambda c: (lambda b, i: (b * nq + i, c))
    in_specs = [pl.BlockSpec((tq, 512), lambda b, i: (b * nq + i, qcol)),
                pl.BlockSpec((pv, 128), prev_map(kcol)), pl.BlockSpec((tq, 128), main_map(kcol)),
                pl.BlockSpec((pv, 128), prev_map(vcol)), pl.BlockSpec((tq, 128), main_map(vcol)),
                pl.BlockSpec((N_KV, N_REP * ATT_BLOCK, L), lambda b, i: (0, 0, 0))]
    args = [qkv, qkv, qkv, qkv, qkv, bias]
    if sinks is not None:
        in_specs.append(pl.BlockSpec(memory_space=pltpu.SMEM))
        args.append(sinks)
    return pl.pallas_call(
        functools.partial(_band_body, npv, sinks is not None),
        grid=(B, nq), in_specs=in_specs,
        out_specs=pl.BlockSpec((tq, 512), lambda b, i: (b * nq + i, 0)),
        out_shape=jax.ShapeDtypeStruct((T, 512), BF16),
        compiler_params=_cparams(("parallel", "parallel")),
        name="band_sink" if sinks is not None else "band_win")(*args)


def _out_body(h_ref, oa_ref, oc_ref, os_ref, ow_ref, gate_ref, ex_ref, wa_ref, wb_ref, wo_ref,
              gn_ref, wrh_ref, wrl_ref, br_ref, hout_ref, rinfo_ref):
    bgs = gate_ref[:, BG_OFF:BG_OFF + 128]
    ob = (jnp.dot(bgs, ex_ref[0], preferred_element_type=F32) * oc_ref[...].astype(F32)
          + jnp.dot(bgs, ex_ref[1], preferred_element_type=F32) * os_ref[...].astype(F32)
          + jnp.dot(bgs, ex_ref[2], preferred_element_type=F32) * ow_ref[...].astype(F32))
    ta = jnp.dot(oa_ref[...], wa_ref[...], preferred_element_type=F32)
    tb = jnp.dot(ob.astype(BF16), wb_ref[...], preferred_element_type=F32)
    merged = (gate_ref[:, 0:D_MODEL].astype(F32) * ta
              + gate_ref[:, D_MODEL:2 * D_MODEL].astype(F32) * tb)
    hn = h_ref[...] + jnp.dot(merged.astype(BF16), wo_ref[...], preferred_element_type=F32)
    _rows_to_tiles(hout_ref, hn)
    xn = _rms(hn, gn_ref[...])
    xh = xn.astype(BF16)
    xl = (xn - xh.astype(F32)).astype(BF16)
    logits = (jnp.dot(xh, wrh_ref[...], preferred_element_type=F32)
              + jnp.dot(xl, wrh_ref[...], preferred_element_type=F32)
              + jnp.dot(xh, wrl_ref[...], preferred_element_type=F32)) + br_ref[...]
    tm = logits.shape[0]
    lane = lax.broadcasted_iota(I32, (1, 128), 1).astype(F32)
    big = 1e9
    is_g = lane < N_GROUPS
    glog = jnp.where(is_g, logits, NEG)
    gmax = jnp.max(glog, axis=-1, keepdims=True)
    gsel = jnp.min(jnp.where(glog == gmax, lane, big), axis=-1, keepdims=True)
    gsum = jnp.sum(jnp.where(is_g, jnp.exp(logits - gmax), 0.0), axis=-1, keepdims=True)
    gw = 1.0 / gsum
    e_lo = N_GROUPS + gsel * EXPERTS_PER_GROUP
    in_g = (lane >= e_lo) & (lane < e_lo + EXPERTS_PER_GROUP)
    ev = jnp.where(in_g, logits, NEG)
    v1 = jnp.max(ev, axis=-1, keepdims=True)
    i1 = jnp.min(jnp.where(ev == v1, lane, big), axis=-1, keepdims=True)
    ev2 = jnp.where(lane == i1, NEG, ev)
    v2 = jnp.max(ev2, axis=-1, keepdims=True)
    i2 = jnp.min(jnp.where(ev2 == v2, lane, big), axis=-1, keepdims=True)
    d = jnp.exp(v2 - v1)
    p1 = 1.0 / (1.0 + d)
    p2 = d / (1.0 + d)
    lane8 = lax.broadcasted_iota(I32, (tm, 8), 1)
    rinfo_ref[...] = jnp.where(lane8 == 0, i1 - N_GROUPS,
                     jnp.where(lane8 == 1, i2 - N_GROUPS,
                     jnp.where(lane8 == 2, p1 * gw,
                     jnp.where(lane8 == 3, p2 * gw, 0.0))))


def _outproj(h, oa, oc, osl, ow, gate, ex, wa, wb, wo, gn, wrh, wrl, br):
    T = h.shape[0]
    tm = ROW_TILE
    row = lambda i: (i, 0)
    c2 = lambda i: (0, 0)
    c3 = lambda i: (0, 0, 0)
    return pl.pallas_call(
        _out_body,
        grid=(T // tm,),
        in_specs=[pl.BlockSpec((tm, D_MODEL), row),
                  pl.BlockSpec((tm, 512), row), pl.BlockSpec((tm, 512), row),
                  pl.BlockSpec((tm, 512), row), pl.BlockSpec((tm, 512), row),
                  pl.BlockSpec((tm, GATE_W), row),
                  pl.BlockSpec((3, 128, 512), c3),
                  pl.BlockSpec((512, D_MODEL), c2), pl.BlockSpec((512, D_MODEL), c2),
                  pl.BlockSpec((D_MODEL, D_MODEL), c2),
                  pl.BlockSpec((1, D_MODEL), c2),
                  pl.BlockSpec((D_MODEL, 128), c2), pl.BlockSpec((D_MODEL, 128), c2),
                  pl.BlockSpec((1, 128), c2)],
        out_specs=[pl.BlockSpec((tm * ROW_CHUNKS, 128), row), pl.BlockSpec((tm, 8), row)],
        out_shape=[jax.ShapeDtypeStruct((T * ROW_CHUNKS, 128), F32), jax.ShapeDtypeStruct((T, 8), F32)],
        compiler_params=_cparams(("parallel",)), name="outproj")(
            h, oa, oc, osl, ow, gate, ex, wa, wb, wo, gn, wrh, wrl, br)


def _moe_body(be_ref, nu_ref, gtok_ref, gtokn_ref, gdst_ref, wrow_ref, gn_ref, w1_ref, w3_ref, w2_ref,
              h_hbm, out_hbm, xbuf, ybuf, gsem, ssem):
    i = pl.program_id(0)
    nu = nu_ref[0]
    tm = xbuf.shape[1] // ROW_CHUNKS
    slot = lax.rem(i, 2)

    def tile_of(row):
        return pl.ds(pl.multiple_of(row * ROW_CHUNKS, ROW_CHUNKS), ROW_CHUNKS)

    def gather_copy(idx_ref, sl, r):
        t = idx_ref[0, 0, r]
        return pltpu.make_async_copy(h_hbm.at[tile_of(t)], xbuf.at[sl, tile_of(r)], gsem.at[sl])

    def scatter_copy(r):
        d = gdst_ref[0, 0, r]
        return pltpu.make_async_copy(ybuf.at[tile_of(r)], out_hbm.at[tile_of(d)], ssem.at[0])

    def start_gather(idx_ref, sl):
        def body(r, c):
            gather_copy(idx_ref, sl, r).start()
            return c
        lax.fori_loop(0, tm, body, 0, unroll=DMA_UNROLL)

    @pl.when(i == 0)
    def _():
        start_gather(gtok_ref, 0)
        ybuf[...] = jnp.zeros(ybuf.shape, F32)
        nfill = tm * ROW_CHUNKS
        fill = pltpu.make_async_copy(ybuf, out_hbm.at[pl.ds(out_hbm.shape[0] - nfill, nfill)], ssem.at[0])
        fill.start()
        fill.wait()

    @pl.when(i + 1 < nu)
    def _():
        start_gather(gtokn_ref, 1 - slot)

    @pl.when(i < nu)
    def _():
        def wbody(r, c):
            gather_copy(gtok_ref, slot, r).wait()
            return c
        lax.fori_loop(0, tm, wbody, 0, unroll=DMA_UNROLL)
        x = _rms(_rows_from_tiles(xbuf.at[slot]), gn_ref[...]).astype(BF16)
        h1 = jnp.dot(x, w1_ref[0], preferred_element_type=F32)
        h3 = jnp.dot(x, w3_ref[0], preferred_element_type=F32)
        act = (h1 * _sigmoid(h1) * h3).astype(BF16)
        y = jnp.dot(act, w2_ref[0], preferred_element_type=F32)
        _rows_to_tiles(ybuf, y * wrow_ref[...])

        def sbody(r, c):
            scatter_copy(r).start()
            return c
        lax.fori_loop(0, tm, sbody, 0, unroll=DMA_UNROLL)

        def swbody(r, c):
            scatter_copy(r).wait()
            return c
        lax.fori_loop(0, tm, swbody, 0, unroll=DMA_UNROLL)


def _moe(h, blk_e, nused, gtok, gdst, wrow, gn, w1, w3, w2):
    T = h.shape[0] // ROW_CHUNKS
    tm = MOE_TM
    nblk = gtok.shape[0]
    grid_spec = pltpu.PrefetchScalarGridSpec(
        num_scalar_prefetch=2,
        grid=(nblk,),
        in_specs=[pl.BlockSpec((1, 1, tm), lambda i, be, nu: (i, 0, 0), memory_space=pltpu.SMEM),
                  pl.BlockSpec((1, 1, tm), lambda i, be, nu: (jnp.minimum(i + 1, nblk - 1), 0, 0),
                               memory_space=pltpu.SMEM),
                  pl.BlockSpec((1, 1, tm), lambda i, be, nu: (i, 0, 0), memory_space=pltpu.SMEM),
                  pl.BlockSpec((tm, 1), lambda i, be, nu: (i, 0)),
                  pl.BlockSpec((1, D_MODEL), lambda i, be, nu: (0, 0)),
                  pl.BlockSpec((1, D_MODEL, EXPERT_FF), lambda i, be, nu: (be[i], 0, 0)),
                  pl.BlockSpec((1, D_MODEL, EXPERT_FF), lambda i, be, nu: (be[i], 0, 0)),
                  pl.BlockSpec((1, EXPERT_FF, D_MODEL), lambda i, be, nu: (be[i], 0, 0)),
                  pl.BlockSpec(memory_space=pl.ANY)],
        out_specs=pl.BlockSpec(memory_space=pl.ANY),
        scratch_shapes=[pltpu.VMEM((2, tm * ROW_CHUNKS, 128), F32),
                        pltpu.VMEM((tm * ROW_CHUNKS, 128), F32),
                        pltpu.SemaphoreType.DMA((2,)),
                        pltpu.SemaphoreType.DMA((1,))])
    return pl.pallas_call(
        _moe_body, grid_spec=grid_spec,
        out_shape=jax.ShapeDtypeStruct(((2 * T + tm) * ROW_CHUNKS, 128), F32),
        compiler_params=_cparams(("arbitrary",)), name="moe")(
            blk_e, nused, gtok, gtok, gdst, wrow, gn, w1, w3, w2, h)


def _route(rinfo, T):
    tm = MOE_TM
    A = 2 * T
    e_flat = rinfo[:, 0:2].astype(I32).T.reshape(A)
    w_flat = rinfo[:, 2:4].T.reshape(A)
    onehot = (e_flat[:, None] == jnp.arange(N_EXPERTS, dtype=I32)[None, :]).astype(I32)
    csum = jnp.cumsum(onehot, axis=0)
    counts = csum[-1]
    padded = (counts + tm - 1) // tm * tm
    pend = jnp.cumsum(padded)
    pstart = pend - padded
    dest = jnp.sum(onehot * (csum - 1 + pstart[None, :]), axis=1)
    P = A + N_EXPERTS * tm
    nblk = P // tm
    a = jnp.arange(A, dtype=I32)
    tok = jnp.where(a >= T, a - T, a)
    upd = jnp.stack([tok, a, lax.bitcast_convert_type(w_flat, I32)], axis=1)
    base = jnp.stack([jnp.zeros((P,), I32), A + jnp.arange(P, dtype=I32) % tm, jnp.zeros((P,), I32)], axis=1)
    packed = base.at[dest].set(upd)
    gtok, gdst = packed[:, 0], packed[:, 1]
    wrow = lax.bitcast_convert_type(packed[:, 2], F32)
    blk_e = jnp.clip(jnp.searchsorted(pend, jnp.arange(nblk, dtype=I32) * tm, side='right'),
                     0, N_EXPERTS - 1).astype(I32)
    nused = (pend[-1] // tm).astype(I32).reshape(1)
    return blk_e, nused, gtok.reshape(nblk, 1, tm), gdst.reshape(nblk, 1, tm), wrow.reshape(P, 1)


def _final_body(h_ref, m0_ref, m1_ref, g_ref, o_ref):
    o_ref[...] = _rms(_rows_from_tiles(h_ref, m0_ref, m1_ref), g_ref[...])


def _final(h, moe, gain):
    T = h.shape[0] // ROW_CHUNKS
    tm = ROW_TILE
    nt = T // tm
    row = lambda i: (i, 0)
    return pl.pallas_call(
        _final_body, grid=(nt,),
        in_specs=[pl.BlockSpec((tm * ROW_CHUNKS, 128), row),
                  pl.BlockSpec((tm * ROW_CHUNKS, 128), row),
                  pl.BlockSpec((tm * ROW_CHUNKS, 128), lambda i: (i + nt, 0)),
                  pl.BlockSpec((1, D_MODEL), lambda i: (0, 0))],
        out_specs=pl.BlockSpec((tm, D_MODEL), row),
        out_shape=jax.ShapeDtypeStruct((T, D_MODEL), F32),
        compiler_params=_cparams(("parallel",)), name="final_norm")(h, moe, moe, gain)


def _bucket_table():
    n = np.arange(MAX_DIST)
    nf = np.maximum(n, 1).astype(np.float64)
    large = MAX_EXACT + (np.log(nf / MAX_EXACT) / math.log(MAX_DIST / MAX_EXACT)
                         * (N_BUCKETS - MAX_EXACT)).astype(np.int64)
    large = np.minimum(large, N_BUCKETS - 1)
    return np.where(n < MAX_EXACT, n, large).astype(np.int32)


def _toeplitz(vals, lo, window, nq, nk, off):
    m = nq + nk
    k = np.arange(m)
    d = off - np.where(k < nk, k, k - m)
    ok = (d >= lo) & (d < window)
    u = jnp.where(ok[None], vals[:, np.clip(d, 0, MAX_DIST - 1)], NEG)
    flat = jnp.tile(u, (1, nq))[:, :nq * (m - 1)]
    return flat.reshape(vals.shape[0], nq, m - 1)[:, :, :nk]


def _band_bias(bias_d, window, npv):
    L = (npv + 1) * ATT_BLOCK
    tile = _toeplitz(bias_d, 0, window, ATT_BLOCK, L, npv * ATT_BLOCK)
    return tile.reshape(N_KV, N_REP * ATT_BLOCK, L)


def _slc_bias(bias_d):
    tq = SEL_TQ
    rel = bias_d - bias_d[:, MAX_DIST - 1:MAX_DIST]
    tile = _toeplitz(rel, 0, 1 << 30, tq, 2 * tq, tq)
    return jnp.transpose(tile.reshape(N_KV, N_REP * tq, 2 * tq), (0, 2, 1))


def _prep_w_in(w):
    s = 0.125
    parts = [w[:, 0:512] * s, w[:, 768:1280] * s, w[:, 512:768], w[:, 1280:2048],
             w[:, 2072:4120], w[:, 2048:2072], jnp.zeros((D_MODEL, GATE_W - 2048 - 24), w.dtype)]
    return jnp.concatenate(parts, axis=1).astype(BF16)


def _gate_expand():
    ex = np.zeros((3, 128, 512), np.float32)
    for c in range(3):
        for h in range(N_HEADS):
            ex[c, h * 3 + c, h * HEAD_DIM:(h + 1) * HEAD_DIM] = 1.0
    return jnp.asarray(ex, BF16)


def _overlap_t(S):
    nc = (S - CMP_LEN) // CMP_STRIDE + 1
    ns = S // SEL_BLOCK
    cstart = np.arange(nc) * CMP_STRIDE
    sstart = np.arange(ns) * SEL_BLOCK
    ov = ((cstart[:, None] < sstart[None, :] + SEL_BLOCK)
          & (cstart[:, None] + CMP_LEN > sstart[None, :])).astype(np.float32)
    ovt = np.zeros((ns, S // CMP_STRIDE), np.float32)
    ovt[:, :nc] = ov.T
    return jnp.asarray(ovt, BF16)


def _block_onehot(S):
    e = (np.arange(S)[:, None] // SEL_BLOCK == np.arange(SEL_BLOCK)[None, :]).astype(np.float32)
    return jnp.asarray(e, BF16)


def kernel(x, rel_bias, norm_mix, w_in, a_sinks, cmp_pos_k, cmp_w1_k, cmp_w2_k, cmp_pos_v, cmp_w1_v,
           cmp_w2_v, w_br_a, w_br_b, w_out, norm_ffn, w_group, b_group, w_expert, b_expert, w1, w3, w2,
           norm_final):
    B, S, D = x.shape
    T = B * S
    depth = w_in.shape[0]
    assert D == D_MODEL and S % BAND_TQ == 0 and S // SEL_BLOCK <= SEL_BLOCK and T % ROW_TILE == 0

    bias_d = rel_bias[_bucket_table()].T.astype(F32)
    bias_a = _band_bias(bias_d[:N_HEADS], A_WINDOW, 1)
    bias_w = _band_bias(bias_d[N_HEADS:], B_WINDOW, 4)
    bias_s = _slc_bias(bias_d[N_HEADS:])
    ex = _gate_expand()
    ovt = _overlap_t(S)
    onehot = jnp.broadcast_to(_block_onehot(S)[None, :, None, :], (B, S, N_KV, SEL_BLOCK))
    half = CMP_STRIDE * HEAD_DIM

    h = x.reshape(T, D)
    moe = None
    for l in range(depth):
        h, qkv, gate = _inproj(h, moe, norm_mix[l].reshape(1, D), _prep_w_in(w_in[l]))

        def rows16(c):
            t = qkv[:, c * 128:(c + 1) * 128].reshape(B, S, N_KV, HEAD_DIM)
            return jnp.transpose(t, (0, 2, 1, 3)).reshape(B, N_KV, S // CMP_STRIDE, half)

        xr = jnp.stack([rows16(COL_BKC), rows16(COL_BVC)])
        pos = jnp.stack([cmp_pos_k[l].reshape(2, half), cmp_pos_v[l].reshape(2, half)])
        cw1 = jnp.stack([cmp_w1_k[l], cmp_w1_v[l]]).astype(BF16)
        cw2 = jnp.stack([cmp_w2_k[l], cmp_w2_v[l]]).astype(BF16)
        kv_cmp = _compress(xr, pos, cw1, cw2)

        o_cmp, selm = _cmpsel(qkv, kv_cmp, kv_cmp, ovt, B, S)
        ks = qkv[:, COL_BKS * 128:(COL_BKS + 1) * 128].reshape(B, S, N_KV, HEAD_DIM)
        kaug = jnp.concatenate([ks, onehot], axis=-1).reshape(T, N_KV * 128)
        nch = S // SEL_TQ
        vs = qkv[:, COL_BVS * 128:(COL_BVS + 1) * 128].reshape(B, nch, SEL_TQ, N_KV, HEAD_DIM)
        vt = jnp.concatenate([jnp.transpose(vs, (0, 1, 3, 4, 2)),
                              jnp.ones((B, nch, N_KV, 1, SEL_TQ), BF16),
                              jnp.zeros((B, nch, N_KV, V_ROWS - HEAD_DIM - 1, SEL_TQ), BF16)], axis=3)
        o_slc = _slc(qkv, selm, kaug, vt, bias_s, B, S)
        o_a = _banded(qkv, bias_a, a_sinks[l], 0, COL_AK, COL_AV, A_WINDOW, B, S)
        o_win = _banded(qkv, bias_w, None, 1, COL_BKW, COL_BVW, B_WINDOW, B, S)

        wr = jnp.concatenate([w_group[l], w_expert[l],
                              jnp.zeros((D, 128 - N_GROUPS - N_EXPERTS), F32)], axis=1)
        wrh = wr.astype(BF16)
        wrl = (wr - wrh.astype(F32)).astype(BF16)
        br = jnp.concatenate([b_group[l], b_expert[l],
                              jnp.zeros((128 - N_GROUPS - N_EXPERTS,), F32)]).reshape(1, 128)
        h, rinfo = _outproj(h, o_a, o_cmp, o_slc, o_win, gate, ex,
                            w_br_a[l].astype(BF16), w_br_b[l].astype(BF16), w_out[l].astype(BF16),
                            norm_ffn[l].reshape(1, D), wrh, wrl, br)

        blk_e, nused, gtok, gdst, wrow = _route(rinfo, T)
        moe = _moe(h, blk_e, nused, gtok, gdst, wrow, norm_ffn[l].reshape(1, D),
                   w1[l].astype(BF16), w3[l].astype(BF16), w2[l].astype(BF16))

    return _final(h, moe, norm_final.reshape(1, D)).reshape(B, S, D)
```

```python
import functools
import math

import numpy as np
import jax
import jax.numpy as jnp
from jax import lax
from jax.experimental import pallas as pl
from jax.experimental.pallas import tpu as pltpu

F32 = jnp.float32
BF16 = jnp.bfloat16
I32 = jnp.int32

D_MODEL = 1024
HEAD_DIM = 64
N_HEADS = 8
N_KV = 2
N_REP = 4
A_WINDOW = 128
B_WINDOW = 512
ATT_BLOCK = 128
CMP_LEN = 32
CMP_STRIDE = 16
CMP_HIDDEN = 256
SEL_BLOCK = 64
SEL_TOPN = 16
N_BUCKETS = 32
MAX_EXACT = 16
MAX_DIST = 128
N_GROUPS = 4
EXPERTS_PER_GROUP = 8
N_EXPERTS = 32
EXPERT_FF = 512
RMS_EPS = 1e-5
NEG = -1e30
FORCE = 1e9
SEL_MASK = -1e9
TAKEN = -3e38

QKV_W = 2048
GATE_W = 2176
BG_OFF = 2048
COL_AK, COL_AV, COL_BKC, COL_BVC, COL_BKS, COL_BVS, COL_BKW, COL_BVW = 8, 9, 10, 11, 12, 13, 14, 15

ROW_TILE = 512
BAND_TQ = 512
SEL_TQ = 256
MOE_TM = 512
DMA_UNROLL = 16
ROW_CHUNKS = D_MODEL // 128
VMEM_LIMIT = 56 * 1024 * 1024

_NT = (((1,), (1,)), ((), ()))


def _cparams(sem):
    return pltpu.CompilerParams(dimension_semantics=sem, vmem_limit_bytes=VMEM_LIMIT)


def _rms(h, g):
    ms = jnp.mean(h * h, axis=-1, keepdims=True)
    return (h * lax.rsqrt(ms + RMS_EPS)) * g


def _sigmoid(z):
    return 1.0 / (1.0 + jnp.exp(-z))


def _stack_heads(ref, r0, r1, c0):
    return jnp.concatenate(
        [ref[r0:r1, c0 + r * HEAD_DIM:c0 + (r + 1) * HEAD_DIM] for r in range(N_REP)], axis=0)


def _rows_from_tiles(*refs):
    n = refs[0].shape[0] // ROW_CHUNKS
    return jnp.concatenate(
        [sum(r[pl.ds(c, n, stride=ROW_CHUNKS), :] for r in refs) for c in range(ROW_CHUNKS)], axis=1)


def _rows_to_tiles(ref, x):
    n = x.shape[0]
    for c in range(ROW_CHUNKS):
        ref[pl.ds(c, n, stride=ROW_CHUNKS), :] = x[:, c * 128:(c + 1) * 128]


def _unstack_heads(o, n):
    return jnp.concatenate([o[r * n:(r + 1) * n] for r in range(N_REP)], axis=1)


def _inproj_body(with_moe, *refs):
    if with_moe:
        h_ref, m0_ref, m1_ref, g_ref, w_ref, hout_ref, qkv_ref, gate_ref = refs
        h = _rows_from_tiles(h_ref, m0_ref, m1_ref)
        hout_ref[...] = h
    else:
        h_ref, g_ref, w_ref, qkv_ref, gate_ref = refs
        h = h_ref[...]
    xb = _rms(h, g_ref[...]).astype(BF16)
    for c0 in range(0, QKV_W, 512):
        acc = jnp.dot(xb, w_ref[:, c0:c0 + 512], preferred_element_type=F32)
        qkv_ref[:, c0:c0 + 512] = acc.astype(BF16)
    for c0 in range(0, GATE_W, 512):
        c1 = min(c0 + 512, GATE_W)
        z = jnp.dot(xb, w_ref[:, QKV_W + c0:QKV_W + c1], preferred_element_type=F32)
        gate_ref[:, c0:c1] = _sigmoid(z).astype(BF16)


def _inproj(h, moe, gain, w_p):
    T = h.shape[0] if moe is None else h.shape[0] // ROW_CHUNKS
    tm = ROW_TILE
    nt = T // tm
    row = lambda i: (i, 0)
    const = lambda i: (0, 0)
    tile = (tm * ROW_CHUNKS, 128)
    in_specs = [pl.BlockSpec((tm, D_MODEL), row) if moe is None else pl.BlockSpec(tile, row)]
    args = [h]
    out_shape = []
    out_specs = []
    if moe is not None:
        in_specs += [pl.BlockSpec(tile, row), pl.BlockSpec(tile, lambda i: (i + nt, 0))]
        args += [moe, moe]
        out_shape.append(jax.ShapeDtypeStruct((T, D_MODEL), F32))
        out_specs.append(pl.BlockSpec((tm, D_MODEL), row))
    in_specs += [pl.BlockSpec((1, D_MODEL), const),
                 pl.BlockSpec((D_MODEL, QKV_W + GATE_W), const)]
    args += [gain, w_p]
    out_shape += [jax.ShapeDtypeStruct((T, QKV_W), BF16), jax.ShapeDtypeStruct((T, GATE_W), BF16)]
    out_specs += [pl.BlockSpec((tm, QKV_W), row), pl.BlockSpec((tm, GATE_W), row)]
    res = pl.pallas_call(
        functools.partial(_inproj_body, moe is not None),
        grid=(nt,), in_specs=in_specs, out_specs=out_specs, out_shape=out_shape,
        compiler_params=_cparams(("parallel",)), name="inproj")(*args)
    if moe is None:
        return h, res[0], res[1]
    return res[0], res[1], res[2]


def _compress_body(x_ref, pos_ref, w1_ref, w2_ref, o_ref):
    half = CMP_STRIDE * HEAD_DIM
    for g in range(N_KV):
        x = x_ref[0, 0, g].astype(F32)
        lo = (x + pos_ref[0, 0:1, :]).astype(BF16)
        hi = (x + pos_ref[0, 1:2, :]).astype(BF16)
        a = jnp.dot(lo, w1_ref[0, 0:half, :], preferred_element_type=F32)
        b = jnp.dot(hi, w1_ref[0, half:2 * half, :], preferred_element_type=F32)
        n = b.shape[0]
        hsum = a + pltpu.roll(b, n - 1, 0)
        hid = jax.nn.gelu(hsum, approximate=True).astype(BF16)
        o_ref[0, 0, g] = jnp.dot(hid, w2_ref[0], preferred_element_type=F32).astype(BF16)


def _compress(xr, pos, w1, w2):
    _, B, G, nr, _ = xr.shape
    return pl.pallas_call(
        _compress_body,
        grid=(2, B),
        in_specs=[pl.BlockSpec((1, 1, G, nr, CMP_STRIDE * HEAD_DIM), lambda k, b: (k, b, 0, 0, 0)),
                  pl.BlockSpec((1, 2, CMP_STRIDE * HEAD_DIM), lambda k, b: (k, 0, 0)),
                  pl.BlockSpec((1, CMP_LEN * HEAD_DIM, CMP_HIDDEN), lambda k, b: (k, 0, 0)),
                  pl.BlockSpec((1, CMP_HIDDEN, HEAD_DIM), lambda k, b: (k, 0, 0))],
        out_specs=pl.BlockSpec((1, 1, G, nr, HEAD_DIM), lambda k, b: (k, b, 0, 0, 0)),
        out_shape=jax.ShapeDtypeStruct((2, B, G, nr, HEAD_DIM), BF16),
        compiler_params=_cparams(("parallel", "parallel")), name="compress")(xr, pos, w1, w2)


def _cmpsel_body(q_ref, kc_ref, vc_ref, ovt_ref, ocmp_ref, selm_ref):
    i = pl.program_id(1)
    tq = q_ref.shape[0]
    nc = kc_ref.shape[3]
    ns = ovt_ref.shape[0]
    t0 = i * tq
    tcol = t0 + lax.broadcasted_iota(I32, (tq, 1), 0)
    ncol = lax.broadcasted_iota(I32, (1, nc), 1)
    cval = (ncol * CMP_STRIDE + (CMP_LEN - 1)) <= tcol
    cval4 = jnp.concatenate([cval] * N_REP, axis=0)
    jrow = lax.broadcasted_iota(I32, (ns, 1), 0)
    trow = t0 + lax.broadcasted_iota(I32, (1, tq), 1)
    tb = lax.shift_right_logical(trow, 6)
    forced = (jrow == 0) | (jrow == tb) | (jrow == tb - 1)
    causal = (jrow * SEL_BLOCK) <= trow
    n_sel = min(SEL_TOPN, ns)
    jrow_f = jrow.astype(F32)
    for g in range(N_KV):
        qs = _stack_heads(q_ref, 0, tq, g * N_REP * HEAD_DIM)
        s = lax.dot_general(qs, kc_ref[0, 0, g], _NT, preferred_element_type=F32)
        sm = jnp.where(cval4, s, NEG)
        m = jnp.max(sm, axis=-1, keepdims=True)
        e = jnp.where(cval4, jnp.exp(sm - m), 0.0)
        den = jnp.sum(e, axis=-1, keepdims=True)
        pc = e / jnp.where(den > 0.0, den, 1.0)
        o = jnp.dot(pc.astype(BF16), vc_ref[0, 0, g], preferred_element_type=F32)
        ocmp_ref[:, g * 256:(g + 1) * 256] = _unstack_heads(o, tq).astype(BF16)
        pcs = pc[0:tq] + pc[tq:2 * tq] + pc[2 * tq:3 * tq] + pc[3 * tq:4 * tq]
        hi = pcs.astype(BF16)
        lo = (pcs - hi.astype(F32)).astype(BF16)
        imp = (lax.dot_general(ovt_ref[...], hi, _NT, preferred_element_type=F32)
               + lax.dot_general(ovt_ref[...], lo, _NT, preferred_element_type=F32))
        imp = jnp.where(causal, jnp.where(forced, FORCE, imp), NEG)
        work = imp
        taken = jnp.zeros((ns, tq), F32)
        for _ in range(n_sel):
            top = jnp.max(work, axis=0, keepdims=True)
            first = jnp.min(jnp.where(work == top, jrow_f, float(ns)), axis=0, keepdims=True)
            hit = jrow_f == first
            taken = jnp.where(hit, 1.0, taken)
            work = jnp.where(hit, TAKEN, work)
        sel = (taken > 0.5) & (imp > NEG * 0.5)
        mt = jnp.where(sel, 0.0, SEL_MASK)
        if ns < 128:
            mt = jnp.concatenate([mt, jnp.zeros((128 - ns, tq), F32)], axis=0)
        selm_ref[:, g * 128:(g + 1) * 128] = mt.T.astype(BF16)


def _cmpsel(qkv, kcmp, vcmp, ovt, B, S):
    T = B * S
    tq = SEL_TQ
    nq = S // tq
    nc = kcmp.shape[3]
    ns = ovt.shape[0]
    return pl.pallas_call(
        _cmpsel_body,
        grid=(B, nq),
        in_specs=[pl.BlockSpec((tq, 512), lambda b, i: (b * nq + i, 1)),
                  pl.BlockSpec((1, 1, N_KV, nc, HEAD_DIM), lambda b, i: (0, b, 0, 0, 0)),
                  pl.BlockSpec((1, 1, N_KV, nc, HEAD_DIM), lambda b, i: (1, b, 0, 0, 0)),
                  pl.BlockSpec((ns, nc), lambda b, i: (0, 0))],
        out_specs=[pl.BlockSpec((tq, 512), lambda b, i: (b * nq + i, 0)),
                   pl.BlockSpec((tq, 256), lambda b, i: (b * nq + i, 0))],
        out_shape=[jax.ShapeDtypeStruct((T, 512), BF16), jax.ShapeDtypeStruct((T, 256), BF16)],
        compiler_params=_cparams(("parallel", "parallel")), name="cmpsel")(qkv, kcmp, vcmp, ovt)


V_ROWS = 80


def _slc_body(q_ref, selm_ref, ka_ref, vt_ref, bias_ref, o_ref, qa_scr, m_scr, acc_scr):
    i = pl.program_id(1)
    tq = q_ref.shape[0]
    prev = jnp.maximum(i - 1, 0)

    def update(s, vt):
        m_old = m_scr[...]
        m_new = jnp.maximum(m_old, jnp.max(s, axis=0, keepdims=True))
        alpha = jnp.exp(m_old - m_new)
        p = jnp.exp(s - m_new).astype(BF16)
        acc_scr[...] = alpha * acc_scr[...] + jnp.dot(vt, p, preferred_element_type=F32)
        m_scr[...] = m_new

    def scores(c, g):
        off = pl.multiple_of(c * tq, tq)
        kt = ka_ref[pl.ds(off, tq), g * 128:(g + 1) * 128]
        return lax.dot_general(kt, qa_scr[...], _NT, preferred_element_type=F32)

    for g in range(N_KV):
        sm = selm_ref[:, g * 128:g * 128 + SEL_BLOCK]
        for r in range(N_REP):
            c0 = g * 256 + r * HEAD_DIM
            qa_scr[r * tq:(r + 1) * tq, :] = jnp.concatenate([q_ref[:, c0:c0 + HEAD_DIM], sm], axis=1)
        m_scr[...] = jnp.full(m_scr.shape, NEG, F32)
        acc_scr[...] = jnp.zeros(acc_scr.shape, F32)
        update(scores(i, g) + bias_ref[g, tq:2 * tq, :], vt_ref[0, i, g])

        @pl.when(i > 0)
        def _():
            update(scores(prev, g) + bias_ref[g, 0:tq, :], vt_ref[0, prev, g])

        def far(c2, carry):
            c = 2 * c2
            off = pl.multiple_of(c * tq, 2 * tq)
            kt = ka_ref[pl.ds(off, 2 * tq), g * 128:(g + 1) * 128]
            s = lax.dot_general(kt, qa_scr[...], _NT, preferred_element_type=F32)
            update(s, jnp.concatenate([vt_ref[0, c, g], vt_ref[0, c + 1, g]], axis=1))
            return carry

        lax.fori_loop(0, lax.shift_right_logical(prev, 1), far, 0)

        @pl.when(lax.rem(prev, 2) == 1)
        def _():
            update(scores(prev - 1, g), vt_ref[0, prev - 1, g])
        acc = acc_scr[...]
        ot = acc[0:HEAD_DIM] / acc[HEAD_DIM:HEAD_DIM + 1]
        pad = jnp.zeros((128 - HEAD_DIM, tq), F32)
        heads = [jnp.concatenate([ot[:, r * tq:(r + 1) * tq], pad], axis=0).T[:, 0:HEAD_DIM]
                 for r in range(N_REP)]
        o_ref[:, g * 256:(g + 1) * 256] = jnp.concatenate(heads, axis=1).astype(BF16)


def _slc(qkv, selm, kaug, vt, bias_near, B, S):
    T = B * S
    tq = SEL_TQ
    nq = S // tq
    return pl.pallas_call(
        _slc_body,
        grid=(B, nq),
        in_specs=[pl.BlockSpec((tq, 512), lambda b, i: (b * nq + i, 1)),
                  pl.BlockSpec((tq, 256), lambda b, i: (b * nq + i, 0)),
                  pl.BlockSpec((S, 256), lambda b, i: (b, 0)),
                  pl.BlockSpec((1, nq, N_KV, V_ROWS, tq), lambda b, i: (b, 0, 0, 0, 0)),
                  pl.BlockSpec((N_KV, 2 * tq, N_REP * tq), lambda b, i: (0, 0, 0))],
        out_specs=pl.BlockSpec((tq, 512), lambda b, i: (b * nq + i, 0)),
        out_shape=jax.ShapeDtypeStruct((T, 512), BF16),
        scratch_shapes=[pltpu.VMEM((N_REP * tq, 128), BF16),
                        pltpu.VMEM((1, N_REP * tq), F32),
                        pltpu.VMEM((V_ROWS, N_REP * tq), F32)],
        compiler_params=_cparams(("parallel", "arbitrary")), name="slc")(qkv, selm, kaug, vt, bias_near)


def _band_body(npv, has_sink, *refs):
    if has_sink:
        q_ref, kp_ref, km_ref, vp_ref, vm_ref, bias_ref, sink_ref, o_ref = refs
    else:
        q_ref, kp_ref, km_ref, vp_ref, vm_ref, bias_ref, o_ref = refs
    i = pl.program_id(1)
    blk = ATT_BLOCK
    L = (npv + 1) * blk
    nsub = q_ref.shape[0] // blk
    col = lax.broadcasted_iota(I32, (1, L), 1)
    for g in range(N_KV):
        ks = slice(g * HEAD_DIM, (g + 1) * HEAD_DIM)
        kfull = jnp.concatenate([kp_ref[:, ks], km_ref[:, ks]], axis=0)
        vfull = jnp.concatenate([vp_ref[:, ks], vm_ref[:, ks]], axis=0)
        for sub in range(nsub):
            qs = _stack_heads(q_ref, sub * blk, (sub + 1) * blk, g * N_REP * HEAD_DIM)
            s = lax.dot_general(qs, kfull[sub * blk:sub * blk + L], _NT,
                                preferred_element_type=F32) + bias_ref[g]
            ncut = (npv - sub) * blk
            if ncut > 0:
                s = jnp.where(jnp.logical_and(col < ncut, i == 0), NEG, s)
            if has_sink:
                s = jnp.where(col == 0, sink_ref[g], s)
            m = jnp.max(s, axis=-1, keepdims=True)
            e = jnp.exp(s - m)
            den = jnp.sum(e, axis=-1, keepdims=True)
            if has_sink:
                e = jnp.where(col == 0, 0.0, e)
            o = jnp.dot(e.astype(BF16), vfull[sub * blk:sub * blk + L],
                        preferred_element_type=F32) / den
            o_ref[sub * blk:(sub + 1) * blk, g * 256:(g + 1) * 256] = _unstack_heads(o, blk).astype(BF16)


def _banded(qkv, bias, sinks, qcol, kcol, vcol, window, B, S):
    T = B * S
    tq = BAND_TQ
    nq = S // tq
    npv = -(-(window - 1) // ATT_BLOCK)
    pv = npv * ATT_BLOCK
    L = pv + ATT_BLOCK
    ratio = tq // pv
    prev_map = lambda c: (lambda b, i: (b * (S // pv) + jnp.maximum(i * ratio - 1, 0), c * (128 // 128)))
    main_map = lambda c: (lambda b, i: (b * nq + i, c))
    in_specs = [pl.BlockSpec((tq, 512), lambda b, i: (b * nq + i, qcol)),
                pl.BlockSpec((pv, 128), prev_map(kcol)), pl.BlockSpec((tq, 128), main_map(kcol)),
                pl.BlockSpec((pv, 128), prev_map(vcol)), pl.BlockSpec((tq, 128), main_map(vcol)),
                pl.BlockSpec((N_KV, N_REP * ATT_BLOCK, L), lambda b, i: (0, 0, 0))]
    args = [qkv, qkv, qkv, qkv, qkv, bias]
    if sinks is not None:
        assert window <= npv * ATT_BLOCK
        in_specs.append(pl.BlockSpec((N_KV, N_REP * ATT_BLOCK, L), lambda b, i: (0, 0, 0)))
        args.append(jnp.broadcast_to(sinks.astype(F32).reshape(N_KV, N_REP, 1, 1),
                                     (N_KV, N_REP, ATT_BLOCK, L)).reshape(N_KV, N_REP * ATT_BLOCK, L))
    return pl.pallas_call(
        functools.partial(_band_body, npv, sinks is not None),
        grid=(B, nq), in_specs=in_specs,
        out_specs=pl.BlockSpec((tq, 512), lambda b, i: (b * nq + i, 0)),
        out_shape=jax.ShapeDtypeStruct((T, 512), BF16),
        compiler_params=_cparams(("parallel", "parallel")),
        name="band_sink" if sinks is not None else "band_win")(*args)


def _out_body(h_ref, oa_ref, oc_ref, os_ref, ow_ref, gate_ref, ex_ref, wa_ref, wb_ref, wo_ref,
              gn_ref, wrh_ref, wrl_ref, br_ref, hout_ref, rinfo_ref):
    bgs = gate_ref[:, BG_OFF:BG_OFF + 128]
    ob = (jnp.dot(bgs, ex_ref[0], preferred_element_type=F32) * oc_ref[...].astype(F32)
          + jnp.dot(bgs, ex_ref[1], preferred_element_type=F32) * os_ref[...].astype(F32)
          + jnp.dot(bgs, ex_ref[2], preferred_element_type=F32) * ow_ref[...].astype(F32))
    ta = jnp.dot(oa_ref[...], wa_ref[...], preferred_element_type=F32)
    tb = jnp.dot(ob.astype(BF16), wb_ref[...], preferred_element_type=F32)
    merged = (gate_ref[:, 0:D_MODEL].astype(F32) * ta
              + gate_ref[:, D_MODEL:2 * D_MODEL].astype(F32) * tb)
    hn = h_ref[...] + jnp.dot(merged.astype(BF16), wo_ref[...], preferred_element_type=F32)
    _rows_to_tiles(hout_ref, hn)
    xn = _rms(hn, gn_ref[...])
    xh = xn.astype(BF16)
    xl = (xn - xh.astype(F32)).astype(BF16)
    logits = (jnp.dot(xh, wrh_ref[...], preferred_element_type=F32)
              + jnp.dot(xl, wrh_ref[...], preferred_element_type=F32)
              + jnp.dot(xh, wrl_ref[...], preferred_element_type=F32)) + br_ref[...]
    tm = logits.shape[0]
    lane = lax.broadcasted_iota(I32, (1, 128), 1).astype(F32)
    big = 1e9
    is_g = lane < N_GROUPS
    glog = jnp.where(is_g, logits, NEG)
    gmax = jnp.max(glog, axis=-1, keepdims=True)
    gsel = jnp.min(jnp.where(glog == gmax, lane, big), axis=-1, keepdims=True)
    gsum = jnp.sum(jnp.where(is_g, jnp.exp(logits - gmax), 0.0), axis=-1, keepdims=True)
    gw = 1.0 / gsum
    e_lo = N_GROUPS + gsel * EXPERTS_PER_GROUP
    in_g = (lane >= e_lo) & (lane < e_lo + EXPERTS_PER_GROUP)
    ev = jnp.where(in_g, logits, NEG)
    v1 = jnp.max(ev, axis=-1, keepdims=True)
    i1 = jnp.min(jnp.where(ev == v1, lane, big), axis=-1, keepdims=True)
    ev2 = jnp.where(lane == i1, NEG, ev)
    v2 = jnp.max(ev2, axis=-1, keepdims=True)
    i2 = jnp.min(jnp.where(ev2 == v2, lane, big), axis=-1, keepdims=True)
    d = jnp.exp(v2 - v1)
    p1 = 1.0 / (1.0 + d)
    p2 = d / (1.0 + d)
    lane8 = lax.broadcasted_iota(I32, (tm, 8), 1)
    rinfo_ref[...] = jnp.where(lane8 == 0, i1 - N_GROUPS,
                     jnp.where(lane8 == 1, i2 - N_GROUPS,
                     jnp.where(lane8 == 2, p1 * gw,
                     jnp.where(lane8 == 3, p2 * gw, 0.0))))


def _outproj(h, oa, oc, osl, ow, gate, ex, wa, wb, wo, gn, wrh, wrl, br):
    T = h.shape[0]
    tm = ROW_TILE
    row = lambda i: (i, 0)
    c2 = lambda i: (0, 0)
    c3 = lambda i: (0, 0, 0)
    return pl.pallas_call(
        _out_body,
        grid=(T // tm,),
        in_specs=[pl.BlockSpec((tm, D_MODEL), row),
                  pl.BlockSpec((tm, 512), row), pl.BlockSpec((tm, 512), row),
                  pl.BlockSpec((tm, 512), row), pl.BlockSpec((tm, 512), row),
                  pl.BlockSpec((tm, GATE_W), row),
                  pl.BlockSpec((3, 128, 512), c3),
                  pl.BlockSpec((512, D_MODEL), c2), pl.BlockSpec((512, D_MODEL), c2),
                  pl.BlockSpec((D_MODEL, D_MODEL), c2),
                  pl.BlockSpec((1, D_MODEL), c2),
                  pl.BlockSpec((D_MODEL, 128), c2), pl.BlockSpec((D_MODEL, 128), c2),
                  pl.BlockSpec((1, 128), c2)],
        out_specs=[pl.BlockSpec((tm * ROW_CHUNKS, 128), row), pl.BlockSpec((tm, 8), row)],
        out_shape=[jax.ShapeDtypeStruct((T * ROW_CHUNKS, 128), F32), jax.ShapeDtypeStruct((T, 8), F32)],
        compiler_params=_cparams(("parallel",)), name="outproj")(
            h, oa, oc, osl, ow, gate, ex, wa, wb, wo, gn, wrh, wrl, br)


def _moe_body(be_ref, nu_ref, gtok_ref, gtokn_ref, gdst_ref, gdstp_ref, wrow_ref, gn_ref,
              w1_ref, w3_ref, w2_ref, h_hbm, out_hbm, xbuf, ybuf, gsem, ssem):
    i = pl.program_id(0)
    nu = nu_ref[0]
    tm = xbuf.shape[1] // ROW_CHUNKS
    slot = lax.rem(i, 2)

    def tile_of(row):
        return pl.ds(pl.multiple_of(row * ROW_CHUNKS, ROW_CHUNKS), ROW_CHUNKS)

    def gather_copy(idx_ref, sl, r):
        t = idx_ref[0, 0, r]
        return pltpu.make_async_copy(h_hbm.at[tile_of(t)], xbuf.at[sl, tile_of(r)], gsem.at[sl])

    def scatter_copy(idx_ref, r):
        d = idx_ref[0, 0, r]
        return pltpu.make_async_copy(ybuf.at[tile_of(r)], out_hbm.at[tile_of(d)], ssem.at[0])

    def drain_scatter(idx_ref):
        def body(r, c):
            scatter_copy(idx_ref, r).wait()
            return c
        lax.fori_loop(0, tm, body, 0, unroll=DMA_UNROLL)

    def start_gather(idx_ref, sl):
        def body(r, c):
            gather_copy(idx_ref, sl, r).start()
            return c
        lax.fori_loop(0, tm, body, 0, unroll=DMA_UNROLL)

    @pl.when(i == 0)
    def _():
        start_gather(gtok_ref, 0)
        ybuf[...] = jnp.zeros(ybuf.shape, F32)
        nfill = tm * ROW_CHUNKS
        fill = pltpu.make_async_copy(ybuf, out_hbm.at[pl.ds(out_hbm.shape[0] - nfill, nfill)], ssem.at[0])
        fill.start()
        fill.wait()

    @pl.when(i + 1 < nu)
    def _():
        start_gather(gtokn_ref, 1 - slot)

    @pl.when(i < nu)
    def _():
        def wbody(r, c):
            gather_copy(gtok_ref, slot, r).wait()
            return c
        lax.fori_loop(0, tm, wbody, 0, unroll=DMA_UNROLL)
        x = _rms(_rows_from_tiles(xbuf.at[slot]), gn_ref[...]).astype(BF16)
        h1 = jnp.dot(x, w1_ref[0], preferred_element_type=F32)
        h3 = jnp.dot(x, w3_ref[0], preferred_element_type=F32)
        act = (h1 * _sigmoid(h1) * h3).astype(BF16)
        y = jnp.dot(act, w2_ref[0], preferred_element_type=F32) * wrow_ref[...]

        @pl.when(i > 0)
        def _():
            drain_scatter(gdstp_ref)

        _rows_to_tiles(ybuf, y)

        def sbody(r, c):
            scatter_copy(gdst_ref, r).start()
            return c
        lax.fori_loop(0, tm, sbody, 0, unroll=DMA_UNROLL)

        @pl.when(i == nu - 1)
        def _():
            drain_scatter(gdst_ref)


def _moe(h, blk_e, nused, gtok, gdst, wrow, gn, w1, w3, w2):
    T = h.shape[0] // ROW_CHUNKS
    tm = MOE_TM
    nblk = gtok.shape[0]
    grid_spec = pltpu.PrefetchScalarGridSpec(
        num_scalar_prefetch=2,
        grid=(nblk,),
        in_specs=[pl.BlockSpec((1, 1, tm), lambda i, be, nu: (i, 0, 0), memory_space=pltpu.SMEM),
                  pl.BlockSpec((1, 1, tm), lambda i, be, nu: (jnp.minimum(i + 1, nblk - 1), 0, 0),
                               memory_space=pltpu.SMEM),
                  pl.BlockSpec((1, 1, tm), lambda i, be, nu: (i, 0, 0), memory_space=pltpu.SMEM),
                  pl.BlockSpec((1, 1, tm), lambda i, be, nu: (jnp.maximum(i - 1, 0), 0, 0),
                               memory_space=pltpu.SMEM),
                  pl.BlockSpec((tm, 1), lambda i, be, nu: (i, 0)),
                  pl.BlockSpec((1, D_MODEL), lambda i, be, nu: (0, 0)),
                  pl.BlockSpec((1, D_MODEL, EXPERT_FF), lambda i, be, nu: (be[i], 0, 0)),
                  pl.BlockSpec((1, D_MODEL, EXPERT_FF), lambda i, be, nu: (be[i], 0, 0)),
                  pl.BlockSpec((1, EXPERT_FF, D_MODEL), lambda i, be, nu: (be[i], 0, 0)),
                  pl.BlockSpec(memory_space=pl.ANY)],
        out_specs=pl.BlockSpec(memory_space=pl.ANY),
        scratch_shapes=[pltpu.VMEM((2, tm * ROW_CHUNKS, 128), F32),
                        pltpu.VMEM((tm * ROW_CHUNKS, 128), F32),
                        pltpu.SemaphoreType.DMA((2,)),
                        pltpu.SemaphoreType.DMA((1,))])
    return pl.pallas_call(
        _moe_body, grid_spec=grid_spec,
        out_shape=jax.ShapeDtypeStruct(((2 * T + tm) * ROW_CHUNKS, 128), F32),
        compiler_params=_cparams(("arbitrary",)), name="moe")(
            blk_e, nused, gtok, gtok, gdst, gdst, wrow, gn, w1, w3, w2, h)


def _route(rinfo, T):
    tm = MOE_TM
    A = 2 * T
    e_flat = rinfo[:, 0:2].astype(I32).T.reshape(A)
    w_flat = rinfo[:, 2:4].T.reshape(A)
    onehot = (e_flat[:, None] == jnp.arange(N_EXPERTS, dtype=I32)[None, :]).astype(I32)
    csum = jnp.cumsum(onehot, axis=0)
    counts = csum[-1]
    padded = (counts + tm - 1) // tm * tm
    pend = jnp.cumsum(padded)
    pstart = pend - padded
    dest = jnp.sum(onehot * (csum - 1 + pstart[None, :]), axis=1)
    P = A + N_EXPERTS * tm
    nblk = P // tm
    a = jnp.arange(A, dtype=I32)
    tok = jnp.where(a >= T, a - T, a)
    upd = jnp.stack([tok, a, lax.bitcast_convert_type(w_flat, I32)], axis=1)
    base = jnp.stack([jnp.zeros((P,), I32), A + jnp.arange(P, dtype=I32) % tm, jnp.zeros((P,), I32)], axis=1)
    packed = base.at[dest].set(upd)
    gtok, gdst = packed[:, 0], packed[:, 1]
    wrow = lax.bitcast_convert_type(packed[:, 2], F32)
    blk_e = jnp.clip(jnp.searchsorted(pend, jnp.arange(nblk, dtype=I32) * tm, side='right'),
                     0, N_EXPERTS - 1).astype(I32)
    nused = (pend[-1] // tm).astype(I32).reshape(1)
    return blk_e, nused, gtok.reshape(nblk, 1, tm), gdst.reshape(nblk, 1, tm), wrow.reshape(P, 1)


def _final_body(h_ref, m0_ref, m1_ref, g_ref, o_ref):
    o_ref[...] = _rms(_rows_from_tiles(h_ref, m0_ref, m1_ref), g_ref[...])


def _final(h, moe, gain):
    T = h.shape[0] // ROW_CHUNKS
    tm = ROW_TILE
    nt = T // tm
    row = lambda i: (i, 0)
    return pl.pallas_call(
        _final_body, grid=(nt,),
        in_specs=[pl.BlockSpec((tm * ROW_CHUNKS, 128), row),
                  pl.BlockSpec((tm * ROW_CHUNKS, 128), row),
                  pl.BlockSpec((tm * ROW_CHUNKS, 128), lambda i: (i + nt, 0)),
                  pl.BlockSpec((1, D_MODEL), lambda i: (0, 0))],
        out_specs=pl.BlockSpec((tm, D_MODEL), row),
        out_shape=jax.ShapeDtypeStruct((T, D_MODEL), F32),
        compiler_params=_cparams(("parallel",)), name="final_norm")(h, moe, moe, gain)


def _bucket_table():
    n = np.arange(MAX_DIST)
    nf = np.maximum(n, 1).astype(np.float64)
    large = MAX_EXACT + (np.log(nf / MAX_EXACT) / math.log(MAX_DIST / MAX_EXACT)
                         * (N_BUCKETS - MAX_EXACT)).astype(np.int64)
    large = np.minimum(large, N_BUCKETS - 1)
    return np.where(n < MAX_EXACT, n, large).astype(np.int32)


def _toeplitz(vals, lo, window, nq, nk, off):
    m = nq + nk
    k = np.arange(m)
    d = off - np.where(k < nk, k, k - m)
    ok = (d >= lo) & (d < window)
    u = jnp.where(ok[None], vals[:, np.clip(d, 0, MAX_DIST - 1)], NEG)
    flat = jnp.tile(u, (1, nq))[:, :nq * (m - 1)]
    return flat.reshape(vals.shape[0], nq, m - 1)[:, :, :nk]


def _band_bias(bias_d, window, npv):
    L = (npv + 1) * ATT_BLOCK
    tile = _toeplitz(bias_d, 0, window, ATT_BLOCK, L, npv * ATT_BLOCK)
    return tile.reshape(N_KV, N_REP * ATT_BLOCK, L)


def _slc_bias(bias_d):
    tq = SEL_TQ
    rel = bias_d - bias_d[:, MAX_DIST - 1:MAX_DIST]
    tile = _toeplitz(rel, 0, 1 << 30, tq, 2 * tq, tq)
    return jnp.transpose(tile.reshape(N_KV, N_REP * tq, 2 * tq), (0, 2, 1))


def _prep_w_in(w):
    s = 0.125
    parts = [w[:, 0:512] * s, w[:, 768:1280] * s, w[:, 512:768], w[:, 1280:2048],
             w[:, 2072:4120], w[:, 2048:2072], jnp.zeros((D_MODEL, GATE_W - 2048 - 24), w.dtype)]
    return jnp.concatenate(parts, axis=1).astype(BF16)


def _gate_expand():
    ex = np.zeros((3, 128, 512), np.float32)
    for c in range(3):
        for h in range(N_HEADS):
            ex[c, h * 3 + c, h * HEAD_DIM:(h + 1) * HEAD_DIM] = 1.0
    return jnp.asarray(ex, BF16)


def _overlap_t(S):
    nc = (S - CMP_LEN) // CMP_STRIDE + 1
    ns = S // SEL_BLOCK
    cstart = np.arange(nc) * CMP_STRIDE
    sstart = np.arange(ns) * SEL_BLOCK
    ov = ((cstart[:, None] < sstart[None, :] + SEL_BLOCK)
          & (cstart[:, None] + CMP_LEN > sstart[None, :])).astype(np.float32)
    ovt = np.zeros((ns, S // CMP_STRIDE), np.float32)
    ovt[:, :nc] = ov.T
    return jnp.asarray(ovt, BF16)


def _block_onehot(S):
    e = (np.arange(S)[:, None] // SEL_BLOCK == np.arange(SEL_BLOCK)[None, :]).astype(np.float32)
    return jnp.asarray(e, BF16)


def kernel(x, rel_bias, norm_mix, w_in, a_sinks, cmp_pos_k, cmp_w1_k, cmp_w2_k, cmp_pos_v, cmp_w1_v,
           cmp_w2_v, w_br_a, w_br_b, w_out, norm_ffn, w_group, b_group, w_expert, b_expert, w1, w3, w2,
           norm_final):
    B, S, D = x.shape
    T = B * S
    depth = w_in.shape[0]
    assert D == D_MODEL and S % BAND_TQ == 0 and S // SEL_BLOCK <= SEL_BLOCK and T % ROW_TILE == 0

    bias_d = rel_bias[_bucket_table()].T.astype(F32)
    bias_a = _band_bias(bias_d[:N_HEADS], A_WINDOW, 1)
    bias_w = _band_bias(bias_d[N_HEADS:], B_WINDOW, 4)
    bias_s = _slc_bias(bias_d[N_HEADS:])
    ex = _gate_expand()
    ovt = _overlap_t(S)
    onehot = jnp.broadcast_to(_block_onehot(S)[None, :, None, :], (B, S, N_KV, SEL_BLOCK))
    half = CMP_STRIDE * HEAD_DIM

    h = x.reshape(T, D)
    moe = None
    for l in range(depth):
        h, qkv, gate = _inproj(h, moe, norm_mix[l].reshape(1, D), _prep_w_in(w_in[l]))

        def rows16(c):
            t = qkv[:, c * 128:(c + 1) * 128].reshape(B, S, N_KV, HEAD_DIM)
            return jnp.transpose(t, (0, 2, 1, 3)).reshape(B, N_KV, S // CMP_STRIDE, half)

        xr = jnp.stack([rows16(COL_BKC), rows16(COL_BVC)])
        pos = jnp.stack([cmp_pos_k[l].reshape(2, half), cmp_pos_v[l].reshape(2, half)])
        cw1 = jnp.stack([cmp_w1_k[l], cmp_w1_v[l]]).astype(BF16)
        cw2 = jnp.stack([cmp_w2_k[l], cmp_w2_v[l]]).astype(BF16)
        kv_cmp = _compress(xr, pos, cw1, cw2)

        o_cmp, selm = _cmpsel(qkv, kv_cmp, kv_cmp, ovt, B, S)
        ks = qkv[:, COL_BKS * 128:(COL_BKS + 1) * 128].reshape(B, S, N_KV, HEAD_DIM)
        kaug = jnp.concatenate([ks, onehot], axis=-1).reshape(T, N_KV * 128)
        nch = S // SEL_TQ
        vs = qkv[:, COL_BVS * 128:(COL_BVS + 1) * 128].reshape(B, nch, SEL_TQ, N_KV, HEAD_DIM)
        vt = jnp.concatenate([jnp.transpose(vs, (0, 1, 3, 4, 2)),
                              jnp.ones((B, nch, N_KV, 1, SEL_TQ), BF16),
                              jnp.zeros((B, nch, N_KV, V_ROWS - HEAD_DIM - 1, SEL_TQ), BF16)], axis=3)
        o_slc = _slc(qkv, selm, kaug, vt, bias_s, B, S)
        o_a = _banded(qkv, bias_a, a_sinks[l], 0, COL_AK, COL_AV, A_WINDOW, B, S)
        o_win = _banded(qkv, bias_w, None, 1, COL_BKW, COL_BVW, B_WINDOW, B, S)

        wr = jnp.concatenate([w_group[l], w_expert[l],
                              jnp.zeros((D, 128 - N_GROUPS - N_EXPERTS), F32)], axis=1)
        wrh = wr.astype(BF16)
        wrl = (wr - wrh.astype(F32)).astype(BF16)
        br = jnp.concatenate([b_group[l], b_expert[l],
                              jnp.zeros((128 - N_GROUPS - N_EXPERTS,), F32)]).reshape(1, 128)
        h, rinfo = _outproj(h, o_a, o_cmp, o_slc, o_win, gate, ex,
                            w_br_a[l].astype(BF16), w_br_b[l].astype(BF16), w_out[l].astype(BF16),
                            norm_ffn[l].reshape(1, D), wrh, wrl, br)

        blk_e, nused, gtok, gdst, wrow = _route(rinfo, T)
        moe = _moe(h, blk_e, nused, gtok, gdst, wrow, norm_ffn[l].reshape(1, D),
                   w1[l].astype(BF16), w3[l].astype(BF16), w2[l].astype(BF16))

    return _final(h, moe, norm_final.reshape(1, D)).reshape(B, S, D)
```

```python
import functools
import math

import numpy as np
import jax
import jax.numpy as jnp
from jax import lax
from jax.experimental import pallas as pl
from jax.experimental.pallas import tpu as pltpu

F32 = jnp.float32
BF16 = jnp.bfloat16
I32 = jnp.int32

D_MODEL = 1024
HEAD_DIM = 64
N_HEADS = 8
N_KV = 2
N_REP = 4
A_WINDOW = 128
B_WINDOW = 512
ATT_BLOCK = 128
CMP_LEN = 32
CMP_STRIDE = 16
CMP_HIDDEN = 256
SEL_BLOCK = 64
SEL_TOPN = 16
N_BUCKETS = 32
MAX_EXACT = 16
MAX_DIST = 128
N_GROUPS = 4
EXPERTS_PER_GROUP = 8
N_EXPERTS = 32
EXPERT_FF = 512
RMS_EPS = 1e-5
NEG = -1e30
FORCE = 1e9
SEL_MASK = -1e9
TAKEN = -3e38

QKV_W = 2048
GATE_W = 2176
BG_OFF = 2048
COL_AK, COL_AV, COL_BKC, COL_BVC, COL_BKS, COL_BVS, COL_BKW, COL_BVW = 8, 9, 10, 11, 12, 13, 14, 15

ROW_TILE = 512
BAND_TQ = 512
SEL_TQ = 256
MOE_TM = 512
DMA_UNROLL = 16
ROW_CHUNKS = D_MODEL // 128
VMEM_LIMIT = 56 * 1024 * 1024

_NT = (((1,), (1,)), ((), ()))


def _cparams(sem):
    return pltpu.CompilerParams(dimension_semantics=sem, vmem_limit_bytes=VMEM_LIMIT)


def _rms(h, g):
    ms = jnp.mean(h * h, axis=-1, keepdims=True)
    return (h * lax.rsqrt(ms + RMS_EPS)) * g


def _sigmoid(z):
    return 1.0 / (1.0 + jnp.exp(-z))


def _stack_heads(ref, r0, r1, c0):
    return jnp.concatenate(
        [ref[r0:r1, c0 + r * HEAD_DIM:c0 + (r + 1) * HEAD_DIM] for r in range(N_REP)], axis=0)


def _rows_from_tiles(*refs):
    n = refs[0].shape[0] // ROW_CHUNKS
    return jnp.concatenate(
        [sum(r[pl.ds(c, n, stride=ROW_CHUNKS), :] for r in refs) for c in range(ROW_CHUNKS)], axis=1)


def _rows_to_tiles(ref, x):
    n = x.shape[0]
    for c in range(ROW_CHUNKS):
        ref[pl.ds(c, n, stride=ROW_CHUNKS), :] = x[:, c * 128:(c + 1) * 128]


def _unstack_heads(o, n):
    return jnp.concatenate([o[r * n:(r + 1) * n] for r in range(N_REP)], axis=1)


def _inproj_body(with_moe, *refs):
    if with_moe:
        h_ref, m0_ref, m1_ref, g_ref, w_ref, hout_ref, qkv_ref, gate_ref = refs
        h = _rows_from_tiles(h_ref, m0_ref, m1_ref)
        hout_ref[...] = h
    else:
        h_ref, g_ref, w_ref, qkv_ref, gate_ref = refs
        h = h_ref[...]
    xb = _rms(h, g_ref[...]).astype(BF16)
    for c0 in range(0, QKV_W, 512):
        acc = jnp.dot(xb, w_ref[:, c0:c0 + 512], preferred_element_type=F32)
        qkv_ref[:, c0:c0 + 512] = acc.astype(BF16)
    for c0 in range(0, GATE_W, 512):
        c1 = min(c0 + 512, GATE_W)
        z = jnp.dot(xb, w_ref[:, QKV_W + c0:QKV_W + c1], preferred_element_type=F32)
        gate_ref[:, c0:c1] = _sigmoid(z).astype(BF16)


def _inproj(h, moe, gain, w_p):
    T = h.shape[0] if moe is None else h.shape[0] // ROW_CHUNKS
    tm = ROW_TILE
    nt = T // tm
    row = lambda i: (i, 0)
    const = lambda i: (0, 0)
    tile = (tm * ROW_CHUNKS, 128)
    in_specs = [pl.BlockSpec((tm, D_MODEL), row) if moe is None else pl.BlockSpec(tile, row)]
    args = [h]
    out_shape = []
    out_specs = []
    if moe is not None:
        in_specs += [pl.BlockSpec(tile, row), pl.BlockSpec(tile, lambda i: (i + nt, 0))]
        args += [moe, moe]
        out_shape.append(jax.ShapeDtypeStruct((T, D_MODEL), F32))
        out_specs.append(pl.BlockSpec((tm, D_MODEL), row))
    in_specs += [pl.BlockSpec((1, D_MODEL), const),
                 pl.BlockSpec((D_MODEL, QKV_W + GATE_W), const)]
    args += [gain, w_p]
    out_shape += [jax.ShapeDtypeStruct((T, QKV_W), BF16), jax.ShapeDtypeStruct((T, GATE_W), BF16)]
    out_specs += [pl.BlockSpec((tm, QKV_W), row), pl.BlockSpec((tm, GATE_W), row)]
    res = pl.pallas_call(
        functools.partial(_inproj_body, moe is not None),
        grid=(nt,), in_specs=in_specs, out_specs=out_specs, out_shape=out_shape,
        compiler_params=_cparams(("parallel",)), name="inproj")(*args)
    if moe is None:
        return h, res[0], res[1]
    return res[0], res[1], res[2]


def _compress_body(x_ref, pos_ref, w1_ref, w2_ref, o_ref):
    half = CMP_STRIDE * HEAD_DIM
    for g in range(N_KV):
        x = x_ref[0, 0, g].astype(F32)
        lo = (x + pos_ref[0, 0:1, :]).astype(BF16)
        hi = (x + pos_ref[0, 1:2, :]).astype(BF16)
        a = jnp.dot(lo, w1_ref[0, 0:half, :], preferred_element_type=F32)
        b = jnp.dot(hi, w1_ref[0, half:2 * half, :], preferred_element_type=F32)
        n = b.shape[0]
        hsum = a + pltpu.roll(b, n - 1, 0)
        hid = jax.nn.gelu(hsum, approximate=True).astype(BF16)
        o_ref[0, 0, g] = jnp.dot(hid, w2_ref[0], preferred_element_type=F32).astype(BF16)


def _compress(xr, pos, w1, w2):
    _, B, G, nr, _ = xr.shape
    return pl.pallas_call(
        _compress_body,
        grid=(2, B),
        in_specs=[pl.BlockSpec((1, 1, G, nr, CMP_STRIDE * HEAD_DIM), lambda k, b: (k, b, 0, 0, 0)),
                  pl.BlockSpec((1, 2, CMP_STRIDE * HEAD_DIM), lambda k, b: (k, 0, 0)),
                  pl.BlockSpec((1, CMP_LEN * HEAD_DIM, CMP_HIDDEN), lambda k, b: (k, 0, 0)),
                  pl.BlockSpec((1, CMP_HIDDEN, HEAD_DIM), lambda k, b: (k, 0, 0))],
        out_specs=pl.BlockSpec((1, 1, G, nr, HEAD_DIM), lambda k, b: (k, b, 0, 0, 0)),
        out_shape=jax.ShapeDtypeStruct((2, B, G, nr, HEAD_DIM), BF16),
        compiler_params=_cparams(("parallel", "parallel")), name="compress")(xr, pos, w1, w2)


def _cmpsel_body(q_ref, kc_ref, vc_ref, ovt_ref, ocmp_ref, selm_ref):
    i = pl.program_id(1)
    tq = q_ref.shape[0]
    nc = kc_ref.shape[3]
    ns = ovt_ref.shape[0]
    t0 = i * tq
    tcol = t0 + lax.broadcasted_iota(I32, (tq, 1), 0)
    ncol = lax.broadcasted_iota(I32, (1, nc), 1)
    cval = (ncol * CMP_STRIDE + (CMP_LEN - 1)) <= tcol
    cval4 = jnp.concatenate([cval] * N_REP, axis=0)
    jrow = lax.broadcasted_iota(I32, (ns, 1), 0)
    trow = t0 + lax.broadcasted_iota(I32, (1, tq), 1)
    tb = lax.shift_right_logical(trow, 6)
    forced = (jrow == 0) | (jrow == tb) | (jrow == tb - 1)
    causal = (jrow * SEL_BLOCK) <= trow
    n_sel = min(SEL_TOPN, ns)
    jrow_f = jrow.astype(F32)
    for g in range(N_KV):
        qs = _stack_heads(q_ref, 0, tq, g * N_REP * HEAD_DIM)
        s = lax.dot_general(qs, kc_ref[0, 0, g], _NT, preferred_element_type=F32)
        sm = jnp.where(cval4, s, NEG)
        m = jnp.max(sm, axis=-1, keepdims=True)
        e = jnp.where(cval4, jnp.exp(sm - m), 0.0)
        den = jnp.sum(e, axis=-1, keepdims=True)
        pc = e / jnp.where(den > 0.0, den, 1.0)
        o = jnp.dot(pc.astype(BF16), vc_ref[0, 0, g], preferred_element_type=F32)
        ocmp_ref[:, g * 256:(g + 1) * 256] = _unstack_heads(o, tq).astype(BF16)
        pcs = pc[0:tq] + pc[tq:2 * tq] + pc[2 * tq:3 * tq] + pc[3 * tq:4 * tq]
        hi = pcs.astype(BF16)
        lo = (pcs - hi.astype(F32)).astype(BF16)
        imp = (lax.dot_general(ovt_ref[...], hi, _NT, preferred_element_type=F32)
               + lax.dot_general(ovt_ref[...], lo, _NT, preferred_element_type=F32))
        imp = jnp.where(causal, jnp.where(forced, FORCE, imp), NEG)
        work = imp
        taken = jnp.zeros((ns, tq), F32)
        for _ in range(n_sel):
            top = jnp.max(work, axis=0, keepdims=True)
            first = jnp.min(jnp.where(work == top, jrow_f, float(ns)), axis=0, keepdims=True)
            hit = jrow_f == first
            taken = jnp.where(hit, 1.0, taken)
            work = jnp.where(hit, TAKEN, work)
        sel = (taken > 0.5) & (imp > NEG * 0.5)
        mt = jnp.where(sel, 0.0, SEL_MASK)
        if ns < 128:
            mt = jnp.concatenate([mt, jnp.zeros((128 - ns, tq), F32)], axis=0)
        selm_ref[:, g * 128:(g + 1) * 128] = mt.T.astype(BF16)


def _cmpsel(qkv, kcmp, vcmp, ovt, B, S):
    T = B * S
    tq = SEL_TQ
    nq = S // tq
    nc = kcmp.shape[3]
    ns = ovt.shape[0]
    return pl.pallas_call(
        _cmpsel_body,
        grid=(B, nq),
        in_specs=[pl.BlockSpec((tq, 512), lambda b, i: (b * nq + i, 1)),
                  pl.BlockSpec((1, 1, N_KV, nc, HEAD_DIM), lambda b, i: (0, b, 0, 0, 0)),
                  pl.BlockSpec((1, 1, N_KV, nc, HEAD_DIM), lambda b, i: (1, b, 0, 0, 0)),
                  pl.BlockSpec((ns, nc), lambda b, i: (0, 0))],
        out_specs=[pl.BlockSpec((tq, 512), lambda b, i: (b * nq + i, 0)),
                   pl.BlockSpec((tq, 256), lambda b, i: (b * nq + i, 0))],
        out_shape=[jax.ShapeDtypeStruct((T, 512), BF16), jax.ShapeDtypeStruct((T, 256), BF16)],
        compiler_params=_cparams(("parallel", "parallel")), name="cmpsel")(qkv, kcmp, vcmp, ovt)


V_ROWS = 80


def _slc_body(q_ref, selm_ref, ka_ref, vt_ref, bias_ref, o_ref, qa_scr, m_scr, acc_scr):
    i = pl.program_id(1)
    tq = q_ref.shape[0]
    prev = jnp.maximum(i - 1, 0)

    def update(s, vt):
        m_old = m_scr[...]
        m_new = jnp.maximum(m_old, jnp.max(s, axis=0, keepdims=True))
        alpha = jnp.exp(m_old - m_new)
        p = jnp.exp(s - m_new).astype(BF16)
        acc_scr[...] = alpha * acc_scr[...] + jnp.dot(vt, p, preferred_element_type=F32)
        m_scr[...] = m_new

    def scores(c, g):
        off = pl.multiple_of(c * tq, tq)
        kt = ka_ref[pl.ds(off, tq), g * 128:(g + 1) * 128]
        return lax.dot_general(kt, qa_scr[...], _NT, preferred_element_type=F32)

    for g in range(N_KV):
        sm = selm_ref[:, g * 128:g * 128 + SEL_BLOCK]
        for r in range(N_REP):
            c0 = g * 256 + r * HEAD_DIM
            qa_scr[r * tq:(r + 1) * tq, :] = jnp.concatenate([q_ref[:, c0:c0 + HEAD_DIM], sm], axis=1)
        m_scr[...] = jnp.full(m_scr.shape, NEG, F32)
        acc_scr[...] = jnp.zeros(acc_scr.shape, F32)
        update(scores(i, g) + bias_ref[g, tq:2 * tq, :], vt_ref[0, i, g])

        @pl.when(i > 0)
        def _():
            update(scores(prev, g) + bias_ref[g, 0:tq, :], vt_ref[0, prev, g])

        def far(c2, carry):
            c = 2 * c2
            off = pl.multiple_of(c * tq, 2 * tq)
            kt = ka_ref[pl.ds(off, 2 * tq), g * 128:(g + 1) * 128]
            s = lax.dot_general(kt, qa_scr[...], _NT, preferred_element_type=F32)
            update(s, jnp.concatenate([vt_ref[0, c, g], vt_ref[0, c + 1, g]], axis=1))
            return carry

        lax.fori_loop(0, lax.shift_right_logical(prev, 1), far, 0)

        @pl.when(lax.rem(prev, 2) == 1)
        def _():
            update(scores(prev - 1, g), vt_ref[0, prev - 1, g])
        acc = acc_scr[...]
        ot = acc[0:HEAD_DIM] / acc[HEAD_DIM:HEAD_DIM + 1]
        pad = jnp.zeros((128 - HEAD_DIM, tq), F32)
        heads = [jnp.concatenate([ot[:, r * tq:(r + 1) * tq], pad], axis=0).T[:, 0:HEAD_DIM]
                 for r in range(N_REP)]
        o_ref[:, g * 256:(g + 1) * 256] = jnp.concatenate(heads, axis=1).astype(BF16)


def _slc(qkv, selm, kaug, vt, bias_near, B, S):
    T = B * S
    tq = SEL_TQ
    nq = S // tq
    return pl.pallas_call(
        _slc_body,
        grid=(B, nq),
        in_specs=[pl.BlockSpec((tq, 512), lambda b, i: (b * nq + i, 1)),
                  pl.BlockSpec((tq, 256), lambda b, i: (b * nq + i, 0)),
                  pl.BlockSpec((S, 256), lambda b, i: (b, 0)),
                  pl.BlockSpec((1, nq, N_KV, V_ROWS, tq), lambda b, i: (b, 0, 0, 0, 0)),
                  pl.BlockSpec((N_KV, 2 * tq, N_REP * tq), lambda b, i: (0, 0, 0))],
        out_specs=pl.BlockSpec((tq, 512), lambda b, i: (b * nq + i, 0)),
        out_shape=jax.ShapeDtypeStruct((T, 512), BF16),
        scratch_shapes=[pltpu.VMEM((N_REP * tq, 128), BF16),
                        pltpu.VMEM((1, N_REP * tq), F32),
                        pltpu.VMEM((V_ROWS, N_REP * tq), F32)],
        compiler_params=_cparams(("parallel", "arbitrary")), name="slc")(qkv, selm, kaug, vt, bias_near)


def _band_body(npv, has_sink, *refs):
    if has_sink:
        q_ref, kp_ref, km_ref, vp_ref, vm_ref, bias_ref, sink_ref, o_ref = refs
    else:
        q_ref, kp_ref, km_ref, vp_ref, vm_ref, bias_ref, o_ref = refs
    i = pl.program_id(1)
    blk = ATT_BLOCK
    L = (npv + 1) * blk
    nsub = q_ref.shape[0] // blk
    col = lax.broadcasted_iota(I32, (1, L), 1)
    for g in range(N_KV):
        ks = slice(g * HEAD_DIM, (g + 1) * HEAD_DIM)
        kfull = jnp.concatenate([kp_ref[:, ks], km_ref[:, ks]], axis=0)
        vfull = jnp.concatenate([vp_ref[:, ks], vm_ref[:, ks]], axis=0)
        for sub in range(nsub):
            qs = _stack_heads(q_ref, sub * blk, (sub + 1) * blk, g * N_REP * HEAD_DIM)
            s = lax.dot_general(qs, kfull[sub * blk:sub * blk + L], _NT,
                                preferred_element_type=F32) + bias_ref[g]
            ncut = (npv - sub) * blk
            if ncut > 0:
                s = jnp.where(jnp.logical_and(col < ncut, i == 0), NEG, s)
            if has_sink:
                s = jnp.where(col == 0, sink_ref[g], s)
            m = jnp.max(s, axis=-1, keepdims=True)
            e = jnp.exp(s - m)
            den = jnp.sum(e, axis=-1, keepdims=True)
            if has_sink:
                e = jnp.where(col == 0, 0.0, e)
            o = jnp.dot(e.astype(BF16), vfull[sub * blk:sub * blk + L],
                        preferred_element_type=F32) / den
            o_ref[sub * blk:(sub + 1) * blk, g * 256:(g + 1) * 256] = _unstack_heads(o, blk).astype(BF16)


def _banded(qkv, bias, sinks, qcol, kcol, vcol, window, B, S):
    T = B * S
    tq = BAND_TQ
    nq = S // tq
    npv = -(-(window - 1) // ATT_BLOCK)
    pv = npv * ATT_BLOCK
    L = pv + ATT_BLOCK
    ratio = tq // pv
    prev_map = lambda c: (lambda b, i: (b * (S // pv) + jnp.maximum(i * ratio - 1, 0), c * (128 // 128)))
    main_map = lambda c: (lambda b, i: (b * nq + i, c))
    in_specs = [pl.BlockSpec((tq, 512), lambda b, i: (b * nq + i, qcol)),
                pl.BlockSpec((pv, 128), prev_map(kcol)), pl.BlockSpec((tq, 128), main_map(kcol)),
                pl.BlockSpec((pv, 128), prev_map(vcol)), pl.BlockSpec((tq, 128), main_map(vcol)),
                pl.BlockSpec((N_KV, N_REP * ATT_BLOCK, L), lambda b, i: (0, 0, 0))]
    args = [qkv, qkv, qkv, qkv, qkv, bias]
    if sinks is not None:
        assert window <= npv * ATT_BLOCK
        in_specs.append(pl.BlockSpec((N_KV, N_REP * ATT_BLOCK, L), lambda b, i: (0, 0, 0)))
        args.append(jnp.broadcast_to(sinks.astype(F32).reshape(N_KV, N_REP, 1, 1),
                                     (N_KV, N_REP, ATT_BLOCK, L)).reshape(N_KV, N_REP * ATT_BLOCK, L))
    return pl.pallas_call(
        functools.partial(_band_body, npv, sinks is not None),
        grid=(B, nq), in_specs=in_specs,
        out_specs=pl.BlockSpec((tq, 512), lambda b, i: (b * nq + i, 0)),
        out_shape=jax.ShapeDtypeStruct((T, 512), BF16),
        compiler_params=_cparams(("parallel", "parallel")),
        name="band_sink" if sinks is not None else "band_win")(*args)


def _out_body(h_ref, oa_ref, oc_ref, os_ref, ow_ref, gate_ref, ex_ref, wa_ref, wb_ref, wo_ref,
              gn_ref, wrh_ref, wrl_ref, br_ref, hout_ref, rinfo_ref):
    bgs = gate_ref[:, BG_OFF:BG_OFF + 128]
    ob = (jnp.dot(bgs, ex_ref[0], preferred_element_type=F32) * oc_ref[...].astype(F32)
          + jnp.dot(bgs, ex_ref[1], preferred_element_type=F32) * os_ref[...].astype(F32)
          + jnp.dot(bgs, ex_ref[2], preferred_element_type=F32) * ow_ref[...].astype(F32))
    ta = jnp.dot(oa_ref[...], wa_ref[...], preferred_element_type=F32)
    tb = jnp.dot(ob.astype(BF16), wb_ref[...], preferred_element_type=F32)
    merged = (gate_ref[:, 0:D_MODEL].astype(F32) * ta
              + gate_ref[:, D_MODEL:2 * D_MODEL].astype(F32) * tb)
    hn = h_ref[...] + jnp.dot(merged.astype(BF16), wo_ref[...], preferred_element_type=F32)
    _rows_to_tiles(hout_ref, hn)
    xn = _rms(hn, gn_ref[...])
    xh = xn.astype(BF16)
    xl = (xn - xh.astype(F32)).astype(BF16)
    logits = (jnp.dot(xh, wrh_ref[...], preferred_element_type=F32)
              + jnp.dot(xl, wrh_ref[...], preferred_element_type=F32)
              + jnp.dot(xh, wrl_ref[...], preferred_element_type=F32)) + br_ref[...]
    tm = logits.shape[0]
    lane = lax.broadcasted_iota(I32, (1, 128), 1).astype(F32)
    big = 1e9
    is_g = lane < N_GROUPS
    glog = jnp.where(is_g, logits, NEG)
    gmax = jnp.max(glog, axis=-1, keepdims=True)
    gsel = jnp.min(jnp.where(glog == gmax, lane, big), axis=-1, keepdims=True)
    gsum = jnp.sum(jnp.where(is_g, jnp.exp(logits - gmax), 0.0), axis=-1, keepdims=True)
    gw = 1.0 / gsum
    e_lo = N_GROUPS + gsel * EXPERTS_PER_GROUP
    in_g = (lane >= e_lo) & (lane < e_lo + EXPERTS_PER_GROUP)
    ev = jnp.where(in_g, logits, NEG)
    v1 = jnp.max(ev, axis=-1, keepdims=True)
    i1 = jnp.min(jnp.where(ev == v1, lane, big), axis=-1, keepdims=True)
    ev2 = jnp.where(lane == i1, NEG, ev)
    v2 = jnp.max(ev2, axis=-1, keepdims=True)
    i2 = jnp.min(jnp.where(ev2 == v2, lane, big), axis=-1, keepdims=True)
    d = jnp.exp(v2 - v1)
    p1 = 1.0 / (1.0 + d)
    p2 = d / (1.0 + d)
    lane8 = lax.broadcasted_iota(I32, (tm, 8), 1)
    rinfo_ref[...] = jnp.where(lane8 == 0, i1 - N_GROUPS,
                     jnp.where(lane8 == 1, i2 - N_GROUPS,
                     jnp.where(lane8 == 2, p1 * gw,
                     jnp.where(lane8 == 3, p2 * gw, 0.0))))


def _outproj(h, oa, oc, osl, ow, gate, ex, wa, wb, wo, gn, wrh, wrl, br):
    T = h.shape[0]
    tm = ROW_TILE
    row = lambda i: (i, 0)
    c2 = lambda i: (0, 0)
    c3 = lambda i: (0, 0, 0)
    return pl.pallas_call(
        _out_body,
        grid=(T // tm,),
        in_specs=[pl.BlockSpec((tm, D_MODEL), row),
                  pl.BlockSpec((tm, 512), row), pl.BlockSpec((tm, 512), row),
                  pl.BlockSpec((tm, 512), row), pl.BlockSpec((tm, 512), row),
                  pl.BlockSpec((tm, GATE_W), row),
                  pl.BlockSpec((3, 128, 512), c3),
                  pl.BlockSpec((512, D_MODEL), c2), pl.BlockSpec((512, D_MODEL), c2),
                  pl.BlockSpec((D_MODEL, D_MODEL), c2),
                  pl.BlockSpec((1, D_MODEL), c2),
                  pl.BlockSpec((D_MODEL, 128), c2), pl.BlockSpec((D_MODEL, 128), c2),
                  pl.BlockSpec((1, 128), c2)],
        out_specs=[pl.BlockSpec((tm * ROW_CHUNKS, 128), row), pl.BlockSpec((tm, 8), row)],
        out_shape=[jax.ShapeDtypeStruct((T * ROW_CHUNKS, 128), F32), jax.ShapeDtypeStruct((T, 8), F32)],
        compiler_params=_cparams(("parallel",)), name="outproj")(
            h, oa, oc, osl, ow, gate, ex, wa, wb, wo, gn, wrh, wrl, br)


def _moe_body(be_ref, nu_ref, gtok_ref, gtokn_ref, gdst_ref, gdstp_ref, wrow_ref, gn_ref,
              w1_ref, w3_ref, w2_ref, h_hbm, out_hbm, xbuf, ybuf, gsem, ssem):
    i = pl.program_id(0)
    nu = nu_ref[0]
    tm = xbuf.shape[1] // ROW_CHUNKS
    slot = lax.rem(i, 2)

    def tile_of(row):
        return pl.ds(pl.multiple_of(row * ROW_CHUNKS, ROW_CHUNKS), ROW_CHUNKS)

    def gather_copy(idx_ref, sl, r):
        t = idx_ref[0, 0, r]
        return pltpu.make_async_copy(h_hbm.at[tile_of(t)], xbuf.at[sl, tile_of(r)], gsem.at[sl])

    def scatter_copy(idx_ref, r):
        d = idx_ref[0, 0, r]
        return pltpu.make_async_copy(ybuf.at[tile_of(r)], out_hbm.at[tile_of(d)], ssem.at[0])

    def drain_scatter(idx_ref):
        def body(r, c):
            scatter_copy(idx_ref, r).wait()
            return c
        lax.fori_loop(0, tm, body, 0, unroll=DMA_UNROLL)

    def start_gather(idx_ref, sl):
        def body(j, c):
            gather_copy(idx_ref, sl, 2 * j).start(priority=0)
            gather_copy(idx_ref, sl, 2 * j + 1).start(priority=1)
            return c
        lax.fori_loop(0, tm // 2, body, 0, unroll=DMA_UNROLL // 2)

    @pl.when(i == 0)
    def _():
        start_gather(gtok_ref, 0)
        ybuf[...] = jnp.zeros(ybuf.shape, F32)
        nfill = tm * ROW_CHUNKS
        fill = pltpu.make_async_copy(ybuf, out_hbm.at[pl.ds(out_hbm.shape[0] - nfill, nfill)], ssem.at[0])
        fill.start()
        fill.wait()

    @pl.when(i + 1 < nu)
    def _():
        start_gather(gtokn_ref, 1 - slot)

    @pl.when(i < nu)
    def _():
        def wbody(r, c):
            gather_copy(gtok_ref, slot, r).wait()
            return c
        lax.fori_loop(0, tm, wbody, 0, unroll=DMA_UNROLL)
        x = _rms(_rows_from_tiles(xbuf.at[slot]), gn_ref[...]).astype(BF16)
        h1 = jnp.dot(x, w1_ref[0], preferred_element_type=F32)
        h3 = jnp.dot(x, w3_ref[0], preferred_element_type=F32)
        act = (h1 * _sigmoid(h1) * h3).astype(BF16)
        y = jnp.dot(act, w2_ref[0], preferred_element_type=F32) * wrow_ref[...]

        @pl.when(i > 0)
        def _():
            drain_scatter(gdstp_ref)

        _rows_to_tiles(ybuf, y)

        def sbody(j, c):
            scatter_copy(gdst_ref, 2 * j).start(priority=0)
            scatter_copy(gdst_ref, 2 * j + 1).start(priority=1)
            return c
        lax.fori_loop(0, tm // 2, sbody, 0, unroll=DMA_UNROLL // 2)

        @pl.when(i == nu - 1)
        def _():
            drain_scatter(gdst_ref)


def _moe(h, blk_e, nused, gtok, gdst, wrow, gn, w1, w3, w2):
    T = h.shape[0] // ROW_CHUNKS
    tm = MOE_TM
    nblk = gtok.shape[0]
    grid_spec = pltpu.PrefetchScalarGridSpec(
        num_scalar_prefetch=2,
        grid=(nblk,),
        in_specs=[pl.BlockSpec((1, 1, tm), lambda i, be, nu: (i, 0, 0), memory_space=pltpu.SMEM),
                  pl.BlockSpec((1, 1, tm), lambda i, be, nu: (jnp.minimum(i + 1, nblk - 1), 0, 0),
                               memory_space=pltpu.SMEM),
                  pl.BlockSpec((1, 1, tm), lambda i, be, nu: (i, 0, 0), memory_space=pltpu.SMEM),
                  pl.BlockSpec((1, 1, tm), lambda i, be, nu: (jnp.maximum(i - 1, 0), 0, 0),
                               memory_space=pltpu.SMEM),
                  pl.BlockSpec((tm, 1), lambda i, be, nu: (i, 0)),
                  pl.BlockSpec((1, D_MODEL), lambda i, be, nu: (0, 0)),
                  pl.BlockSpec((1, D_MODEL, EXPERT_FF), lambda i, be, nu: (be[i], 0, 0)),
                  pl.BlockSpec((1, D_MODEL, EXPERT_FF), lambda i, be, nu: (be[i], 0, 0)),
                  pl.BlockSpec((1, EXPERT_FF, D_MODEL), lambda i, be, nu: (be[i], 0, 0)),
                  pl.BlockSpec(memory_space=pl.ANY)],
        out_specs=pl.BlockSpec(memory_space=pl.ANY),
        scratch_shapes=[pltpu.VMEM((2, tm * ROW_CHUNKS, 128), F32),
                        pltpu.VMEM((tm * ROW_CHUNKS, 128), F32),
                        pltpu.SemaphoreType.DMA((2,)),
                        pltpu.SemaphoreType.DMA((1,))])
    return pl.pallas_call(
        _moe_body, grid_spec=grid_spec,
        out_shape=jax.ShapeDtypeStruct(((2 * T + tm) * ROW_CHUNKS, 128), F32),
        compiler_params=_cparams(("arbitrary",)), name="moe")(
            blk_e, nused, gtok, gtok, gdst, gdst, wrow, gn, w1, w3, w2, h)


def _route(rinfo, T):
    tm = MOE_TM
    A = 2 * T
    e_flat = rinfo[:, 0:2].astype(I32).T.reshape(A)
    w_flat = rinfo[:, 2:4].T.reshape(A)
    onehot = (e_flat[:, None] == jnp.arange(N_EXPERTS, dtype=I32)[None, :]).astype(I32)
    ck = 256
    tri = jnp.asarray(np.tril(np.ones((ck, ck), np.float32)), BF16)
    within = jnp.einsum('ij,cjk->cik', tri, onehot.astype(BF16).reshape(A // ck, ck, N_EXPERTS),
                        preferred_element_type=F32)
    tot = within[:, -1, :]
    csum = (within + (jnp.cumsum(tot, axis=0) - tot)[:, None, :]).reshape(A, N_EXPERTS).astype(I32)
    counts = csum[-1]
    padded = (counts + tm - 1) // tm * tm
    pend = jnp.cumsum(padded)
    pstart = pend - padded
    dest = jnp.sum(onehot * (csum - 1 + pstart[None, :]), axis=1)
    P = A + N_EXPERTS * tm
    nblk = P // tm
    a = jnp.arange(A, dtype=I32)
    tok = jnp.where(a >= T, a - T, a)
    upd = jnp.stack([tok, a, lax.bitcast_convert_type(w_flat, I32)], axis=1)
    base = jnp.stack([jnp.zeros((P,), I32), A + jnp.arange(P, dtype=I32) % tm, jnp.zeros((P,), I32)], axis=1)
    packed = base.at[dest].set(upd, unique_indices=True)
    gtok, gdst = packed[:, 0], packed[:, 1]
    wrow = lax.bitcast_convert_type(packed[:, 2], F32)
    blk_e = jnp.clip(jnp.searchsorted(pend, jnp.arange(nblk, dtype=I32) * tm, side='right'),
                     0, N_EXPERTS - 1).astype(I32)
    nused = (pend[-1] // tm).astype(I32).reshape(1)
    return blk_e, nused, gtok.reshape(nblk, 1, tm), gdst.reshape(nblk, 1, tm), wrow.reshape(P, 1)


def _final_body(h_ref, m0_ref, m1_ref, g_ref, o_ref):
    o_ref[...] = _rms(_rows_from_tiles(h_ref, m0_ref, m1_ref), g_ref[...])


def _final(h, moe, gain):
    T = h.shape[0] // ROW_CHUNKS
    tm = ROW_TILE
    nt = T // tm
    row = lambda i: (i, 0)
    return pl.pallas_call(
        _final_body, grid=(nt,),
        in_specs=[pl.BlockSpec((tm * ROW_CHUNKS, 128), row),
                  pl.BlockSpec((tm * ROW_CHUNKS, 128), row),
                  pl.BlockSpec((tm * ROW_CHUNKS, 128), lambda i: (i + nt, 0)),
                  pl.BlockSpec((1, D_MODEL), lambda i: (0, 0))],
        out_specs=pl.BlockSpec((tm, D_MODEL), row),
        out_shape=jax.ShapeDtypeStruct((T, D_MODEL), F32),
        compiler_params=_cparams(("parallel",)), name="final_norm")(h, moe, moe, gain)


def _bucket_table():
    n = np.arange(MAX_DIST)
    nf = np.maximum(n, 1).astype(np.float64)
    large = MAX_EXACT + (np.log(nf / MAX_EXACT) / math.log(MAX_DIST / MAX_EXACT)
                         * (N_BUCKETS - MAX_EXACT)).astype(np.int64)
    large = np.minimum(large, N_BUCKETS - 1)
    return np.where(n < MAX_EXACT, n, large).astype(np.int32)


def _toeplitz(vals, lo, window, nq, nk, off):
    m = nq + nk
    k = np.arange(m)
    d = off - np.where(k < nk, k, k - m)
    ok = (d >= lo) & (d < window)
    u = jnp.where(ok[None], vals[:, np.clip(d, 0, MAX_DIST - 1)], NEG)
    flat = jnp.tile(u, (1, nq))[:, :nq * (m - 1)]
    return flat.reshape(vals.shape[0], nq, m - 1)[:, :, :nk]


def _band_bias(bias_d, window, npv):
    L = (npv + 1) * ATT_BLOCK
    tile = _toeplitz(bias_d, 0, window, ATT_BLOCK, L, npv * ATT_BLOCK)
    return tile.reshape(N_KV, N_REP * ATT_BLOCK, L)


def _slc_bias(bias_d):
    tq = SEL_TQ
    rel = bias_d - bias_d[:, MAX_DIST - 1:MAX_DIST]
    tile = _toeplitz(rel, 0, 1 << 30, tq, 2 * tq, tq)
    return jnp.transpose(tile.reshape(N_KV, N_REP * tq, 2 * tq), (0, 2, 1))


def _prep_w_in(w):
    s = 0.125
    parts = [w[:, 0:512] * s, w[:, 768:1280] * s, w[:, 512:768], w[:, 1280:2048],
             w[:, 2072:4120], w[:, 2048:2072], jnp.zeros((D_MODEL, GATE_W - 2048 - 24), w.dtype)]
    return jnp.concatenate(parts, axis=1).astype(BF16)


def _gate_expand():
    ex = np.zeros((3, 128, 512), np.float32)
    for c in range(3):
        for h in range(N_HEADS):
            ex[c, h * 3 + c, h * HEAD_DIM:(h + 1) * HEAD_DIM] = 1.0
    return jnp.asarray(ex, BF16)


def _overlap_t(S):
    nc = (S - CMP_LEN) // CMP_STRIDE + 1
    ns = S // SEL_BLOCK
    cstart = np.arange(nc) * CMP_STRIDE
    sstart = np.arange(ns) * SEL_BLOCK
    ov = ((cstart[:, None] < sstart[None, :] + SEL_BLOCK)
          & (cstart[:, None] + CMP_LEN > sstart[None, :])).astype(np.float32)
    ovt = np.zeros((ns, S // CMP_STRIDE), np.float32)
    ovt[:, :nc] = ov.T
    return jnp.asarray(ovt, BF16)


def _block_onehot(S):
    e = (np.arange(S)[:, None] // SEL_BLOCK == np.arange(SEL_BLOCK)[None, :]).astype(np.float32)
    return jnp.asarray(e, BF16)


def kernel(x, rel_bias, norm_mix, w_in, a_sinks, cmp_pos_k, cmp_w1_k, cmp_w2_k, cmp_pos_v, cmp_w1_v,
           cmp_w2_v, w_br_a, w_br_b, w_out, norm_ffn, w_group, b_group, w_expert, b_expert, w1, w3, w2,
           norm_final):
    B, S, D = x.shape
    T = B * S
    depth = w_in.shape[0]
    assert D == D_MODEL and S % BAND_TQ == 0 and S // SEL_BLOCK <= SEL_BLOCK and T % ROW_TILE == 0

    bias_d = rel_bias[_bucket_table()].T.astype(F32)
    bias_a = _band_bias(bias_d[:N_HEADS], A_WINDOW, 1)
    bias_w = _band_bias(bias_d[N_HEADS:], B_WINDOW, 4)
    bias_s = _slc_bias(bias_d[N_HEADS:])
    ex = _gate_expand()
    ovt = _overlap_t(S)
    onehot = jnp.broadcast_to(_block_onehot(S)[None, :, None, :], (B, S, N_KV, SEL_BLOCK))
    half = CMP_STRIDE * HEAD_DIM

    h = x.reshape(T, D)
    moe = None
    for l in range(depth):
        h, qkv, gate = _inproj(h, moe, norm_mix[l].reshape(1, D), _prep_w_in(w_in[l]))

        def rows16(c):
            t = qkv[:, c * 128:(c + 1) * 128].reshape(B, S, N_KV, HEAD_DIM)
            return jnp.transpose(t, (0, 2, 1, 3)).reshape(B, N_KV, S // CMP_STRIDE, half)

        xr = jnp.stack([rows16(COL_BKC), rows16(COL_BVC)])
        pos = jnp.stack([cmp_pos_k[l].reshape(2, half), cmp_pos_v[l].reshape(2, half)])
        cw1 = jnp.stack([cmp_w1_k[l], cmp_w1_v[l]]).astype(BF16)
        cw2 = jnp.stack([cmp_w2_k[l], cmp_w2_v[l]]).astype(BF16)
        kv_cmp = _compress(xr, pos, cw1, cw2)

        o_cmp, selm = _cmpsel(qkv, kv_cmp, kv_cmp, ovt, B, S)
        ks = qkv[:, COL_BKS * 128:(COL_BKS + 1) * 128].reshape(B, S, N_KV, HEAD_DIM)
        kaug = jnp.concatenate([ks, onehot], axis=-1).reshape(T, N_KV * 128)
        nch = S // SEL_TQ
        vs = qkv[:, COL_BVS * 128:(COL_BVS + 1) * 128].reshape(B, nch, SEL_TQ, N_KV, HEAD_DIM)
        vt = jnp.concatenate([jnp.transpose(vs, (0, 1, 3, 4, 2)),
                              jnp.ones((B, nch, N_KV, 1, SEL_TQ), BF16),
                              jnp.zeros((B, nch, N_KV, V_ROWS - HEAD_DIM - 1, SEL_TQ), BF16)], axis=3)
        o_slc = _slc(qkv, selm, kaug, vt, bias_s, B, S)
        o_a = _banded(qkv, bias_a, a_sinks[l], 0, COL_AK, COL_AV, A_WINDOW, B, S)
        o_win = _banded(qkv, bias_w, None, 1, COL_BKW, COL_BVW, B_WINDOW, B, S)

        wr = jnp.concatenate([w_group[l], w_expert[l],
                              jnp.zeros((D, 128 - N_GROUPS - N_EXPERTS), F32)], axis=1)
        wrh = wr.astype(BF16)
        wrl = (wr - wrh.astype(F32)).astype(BF16)
        br = jnp.concatenate([b_group[l], b_expert[l],
                              jnp.zeros((128 - N_GROUPS - N_EXPERTS,), F32)]).reshape(1, 128)
        h, rinfo = _outproj(h, o_a, o_cmp, o_slc, o_win, gate, ex,
                            w_br_a[l].astype(BF16), w_br_b[l].astype(BF16), w_out[l].astype(BF16),
                            norm_ffn[l].reshape(1, D), wrh, wrl, br)

        blk_e, nused, gtok, gdst, wrow = _route(rinfo, T)
        moe = _moe(h, blk_e, nused, gtok, gdst, wrow, norm_ffn[l].reshape(1, D),
                   w1[l].astype(BF16), w3[l].astype(BF16), w2[l].astype(BF16))

    return _final(h, moe, norm_final.reshape(1, D)).reshape(B, S, D)
```

```python
import functools
import math

import numpy as np
import jax
import jax.numpy as jnp
from jax import lax
from jax.experimental import pallas as pl
from jax.experimental.pallas import tpu as pltpu

F32 = jnp.float32
BF16 = jnp.bfloat16
I32 = jnp.int32

D_MODEL = 1024
HEAD_DIM = 64
N_HEADS = 8
N_KV = 2
N_REP = 4
A_WINDOW = 128
B_WINDOW = 512
ATT_BLOCK = 128
CMP_LEN = 32
CMP_STRIDE = 16
CMP_HIDDEN = 256
SEL_BLOCK = 64
SEL_TOPN = 16
N_BUCKETS = 32
MAX_EXACT = 16
MAX_DIST = 128
N_GROUPS = 4
EXPERTS_PER_GROUP = 8
N_EXPERTS = 32
EXPERT_FF = 512
RMS_EPS = 1e-5
NEG = -1e30
FORCE = 1e9
SEL_MASK = -1e9
TAKEN = -3e38

QKV_W = 2048
GATE_W = 2176
BG_OFF = 2048
COL_AK, COL_AV, COL_BKC, COL_BVC, COL_BKS, COL_BVS, COL_BKW, COL_BVW = 8, 9, 10, 11, 12, 13, 14, 15

ROW_TILE = 512
BAND_TQ = 512
SEL_TQ = 256
MOE_TM = 512
DMA_UNROLL = 16
ROW_CHUNKS = D_MODEL // 128
VMEM_LIMIT = 56 * 1024 * 1024

_NT = (((1,), (1,)), ((), ()))


def _cparams(sem):
    return pltpu.CompilerParams(dimension_semantics=sem, vmem_limit_bytes=VMEM_LIMIT)


def _rms(h, g):
    ms = jnp.mean(h * h, axis=-1, keepdims=True)
    return (h * lax.rsqrt(ms + RMS_EPS)) * g


def _sigmoid(z):
    return 1.0 / (1.0 + jnp.exp(-z))


def _stack_heads(ref, r0, r1, c0):
    return jnp.concatenate(
        [ref[r0:r1, c0 + r * HEAD_DIM:c0 + (r + 1) * HEAD_DIM] for r in range(N_REP)], axis=0)


def _rows_from_tiles(*refs):
    n = refs[0].shape[0] // ROW_CHUNKS
    return jnp.concatenate(
        [sum(r[pl.ds(c, n, stride=ROW_CHUNKS), :] for r in refs) for c in range(ROW_CHUNKS)], axis=1)


def _rows_to_tiles(ref, x):
    n = x.shape[0]
    for c in range(ROW_CHUNKS):
        ref[pl.ds(c, n, stride=ROW_CHUNKS), :] = x[:, c * 128:(c + 1) * 128]


def _unstack_heads(o, n):
    return jnp.concatenate([o[r * n:(r + 1) * n] for r in range(N_REP)], axis=1)


def _inproj_body(with_moe, *refs):
    if with_moe:
        h_ref, m0_ref, m1_ref, g_ref, w_ref, hout_ref, qkv_ref, gate_ref = refs
        h = _rows_from_tiles(h_ref, m0_ref, m1_ref)
        hout_ref[...] = h
    else:
        h_ref, g_ref, w_ref, qkv_ref, gate_ref = refs
        h = h_ref[...]
    xb = _rms(h, g_ref[...]).astype(BF16)
    for c0 in range(0, QKV_W, 512):
        acc = jnp.dot(xb, w_ref[:, c0:c0 + 512], preferred_element_type=F32)
        qkv_ref[:, c0:c0 + 512] = acc.astype(BF16)
    for c0 in range(0, GATE_W, 512):
        c1 = min(c0 + 512, GATE_W)
        z = jnp.dot(xb, w_ref[:, QKV_W + c0:QKV_W + c1], preferred_element_type=F32)
        gate_ref[:, c0:c1] = _sigmoid(z).astype(BF16)


def _inproj(h, moe, gain, w_p):
    T = h.shape[0] if moe is None else h.shape[0] // ROW_CHUNKS
    tm = ROW_TILE
    nt = T // tm
    row = lambda i: (i, 0)
    const = lambda i: (0, 0)
    tile = (tm * ROW_CHUNKS, 128)
    in_specs = [pl.BlockSpec((tm, D_MODEL), row) if moe is None else pl.BlockSpec(tile, row)]
    args = [h]
    out_shape = []
    out_specs = []
    if moe is not None:
        in_specs += [pl.BlockSpec(tile, row), pl.BlockSpec(tile, lambda i: (i + nt, 0))]
        args += [moe, moe]
        out_shape.append(jax.ShapeDtypeStruct((T, D_MODEL), F32))
        out_specs.append(pl.BlockSpec((tm, D_MODEL), row))
    in_specs += [pl.BlockSpec((1, D_MODEL), const),
                 pl.BlockSpec((D_MODEL, QKV_W + GATE_W), const)]
    args += [gain, w_p]
    out_shape += [jax.ShapeDtypeStruct((T, QKV_W), BF16), jax.ShapeDtypeStruct((T, GATE_W), BF16)]
    out_specs += [pl.BlockSpec((tm, QKV_W), row), pl.BlockSpec((tm, GATE_W), row)]
    res = pl.pallas_call(
        functools.partial(_inproj_body, moe is not None),
        grid=(nt,), in_specs=in_specs, out_specs=out_specs, out_shape=out_shape,
        compiler_params=_cparams(("parallel",)), name="inproj")(*args)
    if moe is None:
        return h, res[0], res[1]
    return res[0], res[1], res[2]


def _compress_body(x_ref, pos_ref, w1_ref, w2_ref, o_ref):
    half = CMP_STRIDE * HEAD_DIM
    for g in range(N_KV):
        x = x_ref[0, 0, g].astype(F32)
        lo = (x + pos_ref[0, 0:1, :]).astype(BF16)
        hi = (x + pos_ref[0, 1:2, :]).astype(BF16)
        a = jnp.dot(lo, w1_ref[0, 0:half, :], preferred_element_type=F32)
        b = jnp.dot(hi, w1_ref[0, half:2 * half, :], preferred_element_type=F32)
        n = b.shape[0]
        hsum = a + pltpu.roll(b, n - 1, 0)
        hid = jax.nn.gelu(hsum, approximate=True).astype(BF16)
        o_ref[0, 0, g] = jnp.dot(hid, w2_ref[0], preferred_element_type=F32).astype(BF16)


def _compress(xr, pos, w1, w2):
    _, B, G, nr, _ = xr.shape
    return pl.pallas_call(
        _compress_body,
        grid=(2, B),
        in_specs=[pl.BlockSpec((1, 1, G, nr, CMP_STRIDE * HEAD_DIM), lambda k, b: (k, b, 0, 0, 0)),
                  pl.BlockSpec((1, 2, CMP_STRIDE * HEAD_DIM), lambda k, b: (k, 0, 0)),
                  pl.BlockSpec((1, CMP_LEN * HEAD_DIM, CMP_HIDDEN), lambda k, b: (k, 0, 0)),
                  pl.BlockSpec((1, CMP_HIDDEN, HEAD_DIM), lambda k, b: (k, 0, 0))],
        out_specs=pl.BlockSpec((1, 1, G, nr, HEAD_DIM), lambda k, b: (k, b, 0, 0, 0)),
        out_shape=jax.ShapeDtypeStruct((2, B, G, nr, HEAD_DIM), BF16),
        compiler_params=_cparams(("parallel", "parallel")), name="compress")(xr, pos, w1, w2)


def _cmpsel_body(q_ref, kc_ref, vc_ref, ovt_ref, ocmp_ref, selm_ref):
    i = pl.program_id(1)
    tq = q_ref.shape[0]
    nc = kc_ref.shape[3]
    ns = ovt_ref.shape[0]
    t0 = i * tq
    tcol = t0 + lax.broadcasted_iota(I32, (tq, 1), 0)
    ncol = lax.broadcasted_iota(I32, (1, nc), 1)
    cval = (ncol * CMP_STRIDE + (CMP_LEN - 1)) <= tcol
    cval4 = jnp.concatenate([cval] * N_REP, axis=0)
    jrow = lax.broadcasted_iota(I32, (ns, 1), 0)
    trow = t0 + lax.broadcasted_iota(I32, (1, tq), 1)
    tb = lax.shift_right_logical(trow, 6)
    forced = (jrow == 0) | (jrow == tb) | (jrow == tb - 1)
    causal = (jrow * SEL_BLOCK) <= trow
    n_sel = min(SEL_TOPN, ns)
    jrow_f = jrow.astype(F32)
    for g in range(N_KV):
        qs = _stack_heads(q_ref, 0, tq, g * N_REP * HEAD_DIM)
        s = lax.dot_general(qs, kc_ref[0, 0, g], _NT, preferred_element_type=F32)
        sm = jnp.where(cval4, s, NEG)
        m = jnp.max(sm, axis=-1, keepdims=True)
        e = jnp.where(cval4, jnp.exp(sm - m), 0.0)
        den = jnp.sum(e, axis=-1, keepdims=True)
        pc = e / jnp.where(den > 0.0, den, 1.0)
        o = jnp.dot(pc.astype(BF16), vc_ref[0, 0, g], preferred_element_type=F32)
        ocmp_ref[:, g * 256:(g + 1) * 256] = _unstack_heads(o, tq).astype(BF16)
        pcs = pc[0:tq] + pc[tq:2 * tq] + pc[2 * tq:3 * tq] + pc[3 * tq:4 * tq]
        hi = pcs.astype(BF16)
        lo = (pcs - hi.astype(F32)).astype(BF16)
        imp = (lax.dot_general(ovt_ref[...], hi, _NT, preferred_element_type=F32)
               + lax.dot_general(ovt_ref[...], lo, _NT, preferred_element_type=F32))
        imp = jnp.where(causal, jnp.where(forced, FORCE, imp), NEG)
        work = imp
        taken = jnp.zeros((ns, tq), F32)
        for _ in range(n_sel):
            top = jnp.max(work, axis=0, keepdims=True)
            first = jnp.min(jnp.where(work == top, jrow_f, float(ns)), axis=0, keepdims=True)
            hit = jrow_f == first
            taken = jnp.where(hit, 1.0, taken)
            work = jnp.where(hit, TAKEN, work)
        sel = (taken > 0.5) & (imp > NEG * 0.5)
        mt = jnp.where(sel, 0.0, SEL_MASK)
        if ns < 128:
            mt = jnp.concatenate([mt, jnp.zeros((128 - ns, tq), F32)], axis=0)
        selm_ref[:, g * 128:(g + 1) * 128] = mt.T.astype(BF16)


def _cmpsel(qkv, kcmp, vcmp, ovt, B, S):
    T = B * S
    tq = SEL_TQ
    nq = S // tq
    nc = kcmp.shape[3]
    ns = ovt.shape[0]
    return pl.pallas_call(
        _cmpsel_body,
        grid=(B, nq),
        in_specs=[pl.BlockSpec((tq, 512), lambda b, i: (b * nq + i, 1)),
                  pl.BlockSpec((1, 1, N_KV, nc, HEAD_DIM), lambda b, i: (0, b, 0, 0, 0)),
                  pl.BlockSpec((1, 1, N_KV, nc, HEAD_DIM), lambda b, i: (1, b, 0, 0, 0)),
                  pl.BlockSpec((ns, nc), lambda b, i: (0, 0))],
        out_specs=[pl.BlockSpec((tq, 512), lambda b, i: (b * nq + i, 0)),
                   pl.BlockSpec((tq, 256), lambda b, i: (b * nq + i, 0))],
        out_shape=[jax.ShapeDtypeStruct((T, 512), BF16), jax.ShapeDtypeStruct((T, 256), BF16)],
        compiler_params=_cparams(("parallel", "parallel")), name="cmpsel")(qkv, kcmp, vcmp, ovt)


V_ROWS = 80


def _slc_body(q_ref, selm_ref, ka_ref, vt_ref, bias_ref, o_ref, qa_scr, m_scr, acc_scr):
    i = pl.program_id(1)
    tq = q_ref.shape[0]
    prev = jnp.maximum(i - 1, 0)

    def update(s, vt):
        m_old = m_scr[...]
        m_new = jnp.maximum(m_old, jnp.max(s, axis=0, keepdims=True))
        alpha = jnp.exp(m_old - m_new)
        p = jnp.exp(s - m_new).astype(BF16)
        acc_scr[...] = alpha * acc_scr[...] + jnp.dot(vt, p, preferred_element_type=F32)
        m_scr[...] = m_new

    def scores(c, g):
        off = pl.multiple_of(c * tq, tq)
        kt = ka_ref[pl.ds(off, tq), g * 128:(g + 1) * 128]
        return lax.dot_general(kt, qa_scr[...], _NT, preferred_element_type=F32)

    for g in range(N_KV):
        sm = selm_ref[:, g * 128:g * 128 + SEL_BLOCK]
        for r in range(N_REP):
            c0 = g * 256 + r * HEAD_DIM
            qa_scr[r * tq:(r + 1) * tq, :] = jnp.concatenate([q_ref[:, c0:c0 + HEAD_DIM], sm], axis=1)
        m_scr[...] = jnp.full(m_scr.shape, NEG, F32)
        acc_scr[...] = jnp.zeros(acc_scr.shape, F32)
        update(scores(i, g) + bias_ref[g, tq:2 * tq, :], vt_ref[0, i, g])

        @pl.when(i > 0)
        def _():
            update(scores(prev, g) + bias_ref[g, 0:tq, :], vt_ref[0, prev, g])

        def far(c2, carry):
            c = 2 * c2
            off = pl.multiple_of(c * tq, 2 * tq)
            kt = ka_ref[pl.ds(off, 2 * tq), g * 128:(g + 1) * 128]
            s = lax.dot_general(kt, qa_scr[...], _NT, preferred_element_type=F32)
            update(s, jnp.concatenate([vt_ref[0, c, g], vt_ref[0, c + 1, g]], axis=1))
            return carry

        lax.fori_loop(0, lax.shift_right_logical(prev, 1), far, 0)

        @pl.when(lax.rem(prev, 2) == 1)
        def _():
            update(scores(prev - 1, g), vt_ref[0, prev - 1, g])
        acc = acc_scr[...]
        ot = acc[0:HEAD_DIM] / acc[HEAD_DIM:HEAD_DIM + 1]
        pad = jnp.zeros((128 - HEAD_DIM, tq), F32)
        heads = [jnp.concatenate([ot[:, r * tq:(r + 1) * tq], pad], axis=0).T[:, 0:HEAD_DIM]
                 for r in range(N_REP)]
        o_ref[:, g * 256:(g + 1) * 256] = jnp.concatenate(heads, axis=1).astype(BF16)


def _slc(qkv, selm, kaug, vt, bias_near, B, S):
    T = B * S
    tq = SEL_TQ
    nq = S // tq
    return pl.pallas_call(
        _slc_body,
        grid=(B, nq),
        in_specs=[pl.BlockSpec((tq, 512), lambda b, i: (b * nq + i, 1)),
                  pl.BlockSpec((tq, 256), lambda b, i: (b * nq + i, 0)),
                  pl.BlockSpec((S, 256), lambda b, i: (b, 0)),
                  pl.BlockSpec((1, nq, N_KV, V_ROWS, tq), lambda b, i: (b, 0, 0, 0, 0)),
                  pl.BlockSpec((N_KV, 2 * tq, N_REP * tq), lambda b, i: (0, 0, 0))],
        out_specs=pl.BlockSpec((tq, 512), lambda b, i: (b * nq + i, 0)),
        out_shape=jax.ShapeDtypeStruct((T, 512), BF16),
        scratch_shapes=[pltpu.VMEM((N_REP * tq, 128), BF16),
                        pltpu.VMEM((1, N_REP * tq), F32),
                        pltpu.VMEM((V_ROWS, N_REP * tq), F32)],
        compiler_params=_cparams(("parallel", "arbitrary")), name="slc")(qkv, selm, kaug, vt, bias_near)


def _band_body(npv, has_sink, *refs):
    if has_sink:
        q_ref, kp_ref, km_ref, vp_ref, vm_ref, bias_ref, sink_ref, o_ref = refs
    else:
        q_ref, kp_ref, km_ref, vp_ref, vm_ref, bias_ref, o_ref = refs
    i = pl.program_id(1)
    blk = ATT_BLOCK
    L = (npv + 1) * blk
    nsub = q_ref.shape[0] // blk
    col = lax.broadcasted_iota(I32, (1, L), 1)
    for g in range(N_KV):
        ks = slice(g * HEAD_DIM, (g + 1) * HEAD_DIM)
        kfull = jnp.concatenate([kp_ref[:, ks], km_ref[:, ks]], axis=0)
        vfull = jnp.concatenate([vp_ref[:, ks], vm_ref[:, ks]], axis=0)
        for sub in range(nsub):
            qs = _stack_heads(q_ref, sub * blk, (sub + 1) * blk, g * N_REP * HEAD_DIM)
            s = lax.dot_general(qs, kfull[sub * blk:sub * blk + L], _NT,
                                preferred_element_type=F32) + bias_ref[g]
            ncut = (npv - sub) * blk
            if ncut > 0:
                s = jnp.where(jnp.logical_and(col < ncut, i == 0), NEG, s)
            if has_sink:
                s = jnp.where(col == 0, sink_ref[g], s)
            m = jnp.max(s, axis=-1, keepdims=True)
            e = jnp.exp(s - m)
            den = jnp.sum(e, axis=-1, keepdims=True)
            if has_sink:
                e = jnp.where(col == 0, 0.0, e)
            o = jnp.dot(e.astype(BF16), vfull[sub * blk:sub * blk + L],
                        preferred_element_type=F32) / den
            o_ref[sub * blk:(sub + 1) * blk, g * 256:(g + 1) * 256] = _unstack_heads(o, blk).astype(BF16)


def _banded(qkv, bias, sinks, qcol, kcol, vcol, window, B, S):
    T = B * S
    tq = BAND_TQ
    nq = S // tq
    npv = -(-(window - 1) // ATT_BLOCK)
    pv = npv * ATT_BLOCK
    L = pv + ATT_BLOCK
    ratio = tq // pv
    prev_map = lambda c: (lambda b, i: (b * (S // pv) + jnp.maximum(i * ratio - 1, 0), c * (128 // 128)))
    main_map = lambda c: (lambda b, i: (b * nq + i, c))
    in_specs = [pl.BlockSpec((tq, 512), lambda b, i: (b * nq + i, qcol)),
                pl.BlockSpec((pv, 128), prev_map(kcol)), pl.BlockSpec((tq, 128), main_map(kcol)),
                pl.BlockSpec((pv, 128), prev_map(vcol)), pl.BlockSpec((tq, 128), main_map(vcol)),
                pl.BlockSpec((N_KV, N_REP * ATT_BLOCK, L), lambda b, i: (0, 0, 0))]
    args = [qkv, qkv, qkv, qkv, qkv, bias]
    if sinks is not None:
        assert window <= npv * ATT_BLOCK
        in_specs.append(pl.BlockSpec((N_KV, N_REP * ATT_BLOCK, L), lambda b, i: (0, 0, 0)))
        args.append(jnp.broadcast_to(sinks.astype(F32).reshape(N_KV, N_REP, 1, 1),
                                     (N_KV, N_REP, ATT_BLOCK, L)).reshape(N_KV, N_REP * ATT_BLOCK, L))
    return pl.pallas_call(
        functools.partial(_band_body, npv, sinks is not None),
        grid=(B, nq), in_specs=in_specs,
        out_specs=pl.BlockSpec((tq, 512), lambda b, i: (b * nq + i, 0)),
        out_shape=jax.ShapeDtypeStruct((T, 512), BF16),
        compiler_params=_cparams(("parallel", "parallel")),
        name="band_sink" if sinks is not None else "band_win")(*args)


def _out_body(h_ref, oa_ref, oc_ref, os_ref, ow_ref, gate_ref, ex_ref, wa_ref, wb_ref, wo_ref,
              gn_ref, wrh_ref, wrl_ref, br_ref, hout_ref, rinfo_ref):
    bgs = gate_ref[:, BG_OFF:BG_OFF + 128]
    ob = (jnp.dot(bgs, ex_ref[0], preferred_element_type=F32) * oc_ref[...].astype(F32)
          + jnp.dot(bgs, ex_ref[1], preferred_element_type=F32) * os_ref[...].astype(F32)
          + jnp.dot(bgs, ex_ref[2], preferred_element_type=F32) * ow_ref[...].astype(F32))
    ta = jnp.dot(oa_ref[...], wa_ref[...], preferred_element_type=F32)
    tb = jnp.dot(ob.astype(BF16), wb_ref[...], preferred_element_type=F32)
    merged = (gate_ref[:, 0:D_MODEL].astype(F32) * ta
              + gate_ref[:, D_MODEL:2 * D_MODEL].astype(F32) * tb)
    hn = h_ref[...] + jnp.dot(merged.astype(BF16), wo_ref[...], preferred_element_type=F32)
    _rows_to_tiles(hout_ref, hn)
    xn = _rms(hn, gn_ref[...])
    xh = xn.astype(BF16)
    xl = (xn - xh.astype(F32)).astype(BF16)
    logits = (jnp.dot(xh, wrh_ref[...], preferred_element_type=F32)
              + jnp.dot(xl, wrh_ref[...], preferred_element_type=F32)
              + jnp.dot(xh, wrl_ref[...], preferred_element_type=F32)) + br_ref[...]
    tm = logits.shape[0]
    lane = lax.broadcasted_iota(I32, (1, 128), 1).astype(F32)
    big = 1e9
    is_g = lane < N_GROUPS
    glog = jnp.where(is_g, logits, NEG)
    gmax = jnp.max(glog, axis=-1, keepdims=True)
    gsel = jnp.min(jnp.where(glog == gmax, lane, big), axis=-1, keepdims=True)
    gsum = jnp.sum(jnp.where(is_g, jnp.exp(logits - gmax), 0.0), axis=-1, keepdims=True)
    gw = 1.0 / gsum
    e_lo = N_GROUPS + gsel * EXPERTS_PER_GROUP
    in_g = (lane >= e_lo) & (lane < e_lo + EXPERTS_PER_GROUP)
    ev = jnp.where(in_g, logits, NEG)
    v1 = jnp.max(ev, axis=-1, keepdims=True)
    i1 = jnp.min(jnp.where(ev == v1, lane, big), axis=-1, keepdims=True)
    ev2 = jnp.where(lane == i1, NEG, ev)
    v2 = jnp.max(ev2, axis=-1, keepdims=True)
    i2 = jnp.min(jnp.where(ev2 == v2, lane, big), axis=-1, keepdims=True)
    d = jnp.exp(v2 - v1)
    p1 = 1.0 / (1.0 + d)
    p2 = d / (1.0 + d)
    lane8 = lax.broadcasted_iota(I32, (tm, 8), 1)
    rinfo_ref[...] = jnp.where(lane8 == 0, i1 - N_GROUPS,
                     jnp.where(lane8 == 1, i2 - N_GROUPS,
                     jnp.where(lane8 == 2, p1 * gw,
                     jnp.where(lane8 == 3, p2 * gw, 0.0))))


def _outproj(h, oa, oc, osl, ow, gate, ex, wa, wb, wo, gn, wrh, wrl, br):
    T = h.shape[0]
    tm = ROW_TILE
    row = lambda i: (i, 0)
    c2 = lambda i: (0, 0)
    c3 = lambda i: (0, 0, 0)
    return pl.pallas_call(
        _out_body,
        grid=(T // tm,),
        in_specs=[pl.BlockSpec((tm, D_MODEL), row),
                  pl.BlockSpec((tm, 512), row), pl.BlockSpec((tm, 512), row),
                  pl.BlockSpec((tm, 512), row), pl.BlockSpec((tm, 512), row),
                  pl.BlockSpec((tm, GATE_W), row),
                  pl.BlockSpec((3, 128, 512), c3),
                  pl.BlockSpec((512, D_MODEL), c2), pl.BlockSpec((512, D_MODEL), c2),
                  pl.BlockSpec((D_MODEL, D_MODEL), c2),
                  pl.BlockSpec((1, D_MODEL), c2),
                  pl.BlockSpec((D_MODEL, 128), c2), pl.BlockSpec((D_MODEL, 128), c2),
                  pl.BlockSpec((1, 128), c2)],
        out_specs=[pl.BlockSpec((tm * ROW_CHUNKS, 128), row), pl.BlockSpec((tm, 8), row)],
        out_shape=[jax.ShapeDtypeStruct((T * ROW_CHUNKS, 128), F32), jax.ShapeDtypeStruct((T, 8), F32)],
        compiler_params=_cparams(("parallel",)), name="outproj")(
            h, oa, oc, osl, ow, gate, ex, wa, wb, wo, gn, wrh, wrl, br)


def _moe_body(be_ref, nu_ref, gtok_ref, gtokn_ref, gdst_ref, gdstp_ref, wrow_ref, gn_ref,
              w1_ref, w3_ref, w2_ref, h_hbm, out_hbm, xbuf, ybuf, gsem, ssem):
    i = pl.program_id(0)
    nu = nu_ref[0]
    tm = xbuf.shape[1] // ROW_CHUNKS
    slot = lax.rem(i, 2)

    def tile_of(row):
        return pl.ds(pl.multiple_of(row * ROW_CHUNKS, ROW_CHUNKS), ROW_CHUNKS)

    def gather_copy(idx_ref, sl, r):
        t = idx_ref[0, 0, r]
        return pltpu.make_async_copy(h_hbm.at[tile_of(t)], xbuf.at[sl, tile_of(r)], gsem.at[sl])

    def scatter_copy(idx_ref, r):
        d = idx_ref[0, 0, r]
        return pltpu.make_async_copy(ybuf.at[tile_of(r)], out_hbm.at[tile_of(d)], ssem.at[0])

    def drain_scatter(idx_ref):
        def body(r, c):
            scatter_copy(idx_ref, r).wait()
            return c
        lax.fori_loop(0, tm, body, 0, unroll=DMA_UNROLL)

    def start_gather(idx_ref, sl):
        def body(j, c):
            gather_copy(idx_ref, sl, 2 * j).start(priority=0)
            gather_copy(idx_ref, sl, 2 * j + 1).start(priority=1)
            return c
        lax.fori_loop(0, tm // 2, body, 0, unroll=DMA_UNROLL // 2)

    @pl.when(i == 0)
    def _():
        start_gather(gtok_ref, 0)
        ybuf[...] = jnp.zeros(ybuf.shape, F32)
        nfill = tm * ROW_CHUNKS
        fill = pltpu.make_async_copy(ybuf, out_hbm.at[pl.ds(out_hbm.shape[0] - nfill, nfill)], ssem.at[0])
        fill.start()
        fill.wait()

    @pl.when(i + 1 < nu)
    def _():
        start_gather(gtokn_ref, 1 - slot)

    @pl.when(i < nu)
    def _():
        def wbody(r, c):
            gather_copy(gtok_ref, slot, r).wait()
            return c
        lax.fori_loop(0, tm, wbody, 0, unroll=DMA_UNROLL)
        x = _rms(_rows_from_tiles(xbuf.at[slot]), gn_ref[...]).astype(BF16)
        h1 = jnp.dot(x, w1_ref[0], preferred_element_type=F32)
        h3 = jnp.dot(x, w3_ref[0], preferred_element_type=F32)
        act = (h1 * _sigmoid(h1) * h3).astype(BF16)
        y = jnp.dot(act, w2_ref[0], preferred_element_type=F32) * wrow_ref[...]

        @pl.when(i > 0)
        def _():
            drain_scatter(gdstp_ref)

        _rows_to_tiles(ybuf, y)

        def sbody(j, c):
            scatter_copy(gdst_ref, 2 * j).start(priority=0)
            scatter_copy(gdst_ref, 2 * j + 1).start(priority=1)
            return c
        lax.fori_loop(0, tm // 2, sbody, 0, unroll=DMA_UNROLL // 2)

        @pl.when(i == nu - 1)
        def _():
            drain_scatter(gdst_ref)


def _moe(h, blk_e, nused, gtok, gdst, wrow, gn, w1, w3, w2):
    T = h.shape[0] // ROW_CHUNKS
    tm = MOE_TM
    nblk = gtok.shape[0]
    grid_spec = pltpu.PrefetchScalarGridSpec(
        num_scalar_prefetch=2,
        grid=(nblk,),
        in_specs=[pl.BlockSpec((1, 1, tm), lambda i, be, nu: (i, 0, 0), memory_space=pltpu.SMEM),
                  pl.BlockSpec((1, 1, tm), lambda i, be, nu: (jnp.minimum(i + 1, nblk - 1), 0, 0),
                               memory_space=pltpu.SMEM),
                  pl.BlockSpec((1, 1, tm), lambda i, be, nu: (i, 0, 0), memory_space=pltpu.SMEM),
                  pl.BlockSpec((1, 1, tm), lambda i, be, nu: (jnp.maximum(i - 1, 0), 0, 0),
                               memory_space=pltpu.SMEM),
                  pl.BlockSpec((tm, 1), lambda i, be, nu: (i, 0)),
                  pl.BlockSpec((1, D_MODEL), lambda i, be, nu: (0, 0)),
                  pl.BlockSpec((1, D_MODEL, EXPERT_FF), lambda i, be, nu: (be[i], 0, 0)),
                  pl.BlockSpec((1, D_MODEL, EXPERT_FF), lambda i, be, nu: (be[i], 0, 0)),
                  pl.BlockSpec((1, EXPERT_FF, D_MODEL), lambda i, be, nu: (be[i], 0, 0)),
                  pl.BlockSpec(memory_space=pl.ANY)],
        out_specs=pl.BlockSpec(memory_space=pl.ANY),
        scratch_shapes=[pltpu.VMEM((2, tm * ROW_CHUNKS, 128), F32),
                        pltpu.VMEM((tm * ROW_CHUNKS, 128), F32),
                        pltpu.SemaphoreType.DMA((2,)),
                        pltpu.SemaphoreType.DMA((1,))])
    return pl.pallas_call(
        _moe_body, grid_spec=grid_spec,
        out_shape=jax.ShapeDtypeStruct(((2 * T + tm) * ROW_CHUNKS, 128), F32),
        compiler_params=_cparams(("arbitrary",)), name="moe")(
            blk_e, nused, gtok, gtok, gdst, gdst, wrow, gn, w1, w3, w2, h)


def _route(rinfo, T):
    tm = MOE_TM
    A = 2 * T
    e_flat = rinfo[:, 0:2].astype(I32).T.reshape(A)
    w_flat = rinfo[:, 2:4].T.reshape(A)
    onehot = (e_flat[:, None] == jnp.arange(N_EXPERTS, dtype=I32)[None, :]).astype(I32)
    ck = 256
    tri = jnp.asarray(np.tril(np.ones((ck, ck), np.float32)), BF16)
    within = jnp.einsum('ij,cjk->cik', tri, onehot.astype(BF16).reshape(A // ck, ck, N_EXPERTS),
                        preferred_element_type=F32)
    tot = within[:, -1, :]
    csum = (within + (jnp.cumsum(tot, axis=0) - tot)[:, None, :]).reshape(A, N_EXPERTS).astype(I32)
    counts = csum[-1]
    padded = (counts + tm - 1) // tm * tm
    pend = jnp.cumsum(padded)
    pstart = pend - padded
    dest = jnp.sum(onehot * (csum - 1 + pstart[None, :]), axis=1)
    P = A + N_EXPERTS * tm
    nblk = P // tm
    a = jnp.arange(A, dtype=I32)
    tok = jnp.where(a >= T, a - T, a)
    upd = jnp.stack([tok, a, lax.bitcast_convert_type(w_flat, I32)], axis=1)
    base = jnp.stack([jnp.zeros((P,), I32), A + jnp.arange(P, dtype=I32) % tm, jnp.zeros((P,), I32)], axis=1)
    packed = base.at[dest].set(upd, unique_indices=True)
    gtok, gdst = packed[:, 0], packed[:, 1]
    wrow = lax.bitcast_convert_type(packed[:, 2], F32)
    starts = jnp.arange(nblk, dtype=I32) * tm
    blk_e = jnp.minimum(jnp.sum((pend[None, :] <= starts[:, None]).astype(I32), axis=1),
                        N_EXPERTS - 1)
    nused = (pend[-1] // tm).astype(I32).reshape(1)
    return blk_e, nused, gtok.reshape(nblk, 1, tm), gdst.reshape(nblk, 1, tm), wrow.reshape(P, 1)


def _final_body(h_ref, m0_ref, m1_ref, g_ref, o_ref):
    o_ref[...] = _rms(_rows_from_tiles(h_ref, m0_ref, m1_ref), g_ref[...])


def _final(h, moe, gain):
    T = h.shape[0] // ROW_CHUNKS
    tm = ROW_TILE
    nt = T // tm
    row = lambda i: (i, 0)
    return pl.pallas_call(
        _final_body, grid=(nt,),
        in_specs=[pl.BlockSpec((tm * ROW_CHUNKS, 128), row),
                  pl.BlockSpec((tm * ROW_CHUNKS, 128), row),
                  pl.BlockSpec((tm * ROW_CHUNKS, 128), lambda i: (i + nt, 0)),
                  pl.BlockSpec((1, D_MODEL), lambda i: (0, 0))],
        out_specs=pl.BlockSpec((tm, D_MODEL), row),
        out_shape=jax.ShapeDtypeStruct((T, D_MODEL), F32),
        compiler_params=_cparams(("parallel",)), name="final_norm")(h, moe, moe, gain)


def _bucket_table():
    n = np.arange(MAX_DIST)
    nf = np.maximum(n, 1).astype(np.float64)
    large = MAX_EXACT + (np.log(nf / MAX_EXACT) / math.log(MAX_DIST / MAX_EXACT)
                         * (N_BUCKETS - MAX_EXACT)).astype(np.int64)
    large = np.minimum(large, N_BUCKETS - 1)
    return np.where(n < MAX_EXACT, n, large).astype(np.int32)


def _toeplitz(vals, lo, window, nq, nk, off):
    m = nq + nk
    k = np.arange(m)
    d = off - np.where(k < nk, k, k - m)
    ok = (d >= lo) & (d < window)
    u = jnp.where(ok[None], vals[:, np.clip(d, 0, MAX_DIST - 1)], NEG)
    flat = jnp.tile(u, (1, nq))[:, :nq * (m - 1)]
    return flat.reshape(vals.shape[0], nq, m - 1)[:, :, :nk]


def _band_bias(bias_d, window, npv):
    L = (npv + 1) * ATT_BLOCK
    tile = _toeplitz(bias_d, 0, window, ATT_BLOCK, L, npv * ATT_BLOCK)
    return tile.reshape(N_KV, N_REP * ATT_BLOCK, L)


def _slc_bias(bias_d):
    tq = SEL_TQ
    rel = bias_d - bias_d[:, MAX_DIST - 1:MAX_DIST]
    tile = _toeplitz(rel, 0, 1 << 30, tq, 2 * tq, tq)
    return jnp.transpose(tile.reshape(N_KV, N_REP * tq, 2 * tq), (0, 2, 1))


def _prep_w_in(w):
    s = 0.125
    parts = [w[:, 0:512] * s, w[:, 768:1280] * s, w[:, 512:768], w[:, 1280:2048],
             w[:, 2072:4120], w[:, 2048:2072], jnp.zeros((D_MODEL, GATE_W - 2048 - 24), w.dtype)]
    return jnp.concatenate(parts, axis=1).astype(BF16)


def _gate_expand():
    ex = np.zeros((3, 128, 512), np.float32)
    for c in range(3):
        for h in range(N_HEADS):
            ex[c, h * 3 + c, h * HEAD_DIM:(h + 1) * HEAD_DIM] = 1.0
    return jnp.asarray(ex, BF16)


def _overlap_t(S):
    nc = (S - CMP_LEN) // CMP_STRIDE + 1
    ns = S // SEL_BLOCK
    cstart = np.arange(nc) * CMP_STRIDE
    sstart = np.arange(ns) * SEL_BLOCK
    ov = ((cstart[:, None] < sstart[None, :] + SEL_BLOCK)
          & (cstart[:, None] + CMP_LEN > sstart[None, :])).astype(np.float32)
    ovt = np.zeros((ns, S // CMP_STRIDE), np.float32)
    ovt[:, :nc] = ov.T
    return jnp.asarray(ovt, BF16)


def _block_onehot(S):
    e = (np.arange(S)[:, None] // SEL_BLOCK == np.arange(SEL_BLOCK)[None, :]).astype(np.float32)
    return jnp.asarray(e, BF16)


def kernel(x, rel_bias, norm_mix, w_in, a_sinks, cmp_pos_k, cmp_w1_k, cmp_w2_k, cmp_pos_v, cmp_w1_v,
           cmp_w2_v, w_br_a, w_br_b, w_out, norm_ffn, w_group, b_group, w_expert, b_expert, w1, w3, w2,
           norm_final):
    B, S, D = x.shape
    T = B * S
    depth = w_in.shape[0]
    assert D == D_MODEL and S % BAND_TQ == 0 and S // SEL_BLOCK <= SEL_BLOCK and T % ROW_TILE == 0

    bias_d = rel_bias[_bucket_table()].T.astype(F32)
    bias_a = _band_bias(bias_d[:N_HEADS], A_WINDOW, 1)
    bias_w = _band_bias(bias_d[N_HEADS:], B_WINDOW, 4)
    bias_s = _slc_bias(bias_d[N_HEADS:])
    ex = _gate_expand()
    ovt = _overlap_t(S)
    onehot = jnp.broadcast_to(_block_onehot(S)[None, :, None, :], (B, S, N_KV, SEL_BLOCK))
    half = CMP_STRIDE * HEAD_DIM

    h = x.reshape(T, D)
    moe = None
    for l in range(depth):
        h, qkv, gate = _inproj(h, moe, norm_mix[l].reshape(1, D), _prep_w_in(w_in[l]))

        def rows16(c):
            t = qkv[:, c * 128:(c + 1) * 128].reshape(B, S, N_KV, HEAD_DIM)
            return jnp.transpose(t, (0, 2, 1, 3)).reshape(B, N_KV, S // CMP_STRIDE, half)

        xr = jnp.stack([rows16(COL_BKC), rows16(COL_BVC)])
        pos = jnp.stack([cmp_pos_k[l].reshape(2, half), cmp_pos_v[l].reshape(2, half)])
        cw1 = jnp.stack([cmp_w1_k[l], cmp_w1_v[l]]).astype(BF16)
        cw2 = jnp.stack([cmp_w2_k[l], cmp_w2_v[l]]).astype(BF16)
        kv_cmp = _compress(xr, pos, cw1, cw2)

        o_cmp, selm = _cmpsel(qkv, kv_cmp, kv_cmp, ovt, B, S)
        ks = qkv[:, COL_BKS * 128:(COL_BKS + 1) * 128].reshape(B, S, N_KV, HEAD_DIM)
        kaug = jnp.concatenate([ks, onehot], axis=-1).reshape(T, N_KV * 128)
        nch = S // SEL_TQ
        vs = qkv[:, COL_BVS * 128:(COL_BVS + 1) * 128].reshape(B, nch, SEL_TQ, N_KV, HEAD_DIM)
        vt = jnp.concatenate([jnp.transpose(vs, (0, 1, 3, 4, 2)),
                              jnp.ones((B, nch, N_KV, 1, SEL_TQ), BF16),
                              jnp.zeros((B, nch, N_KV, V_ROWS - HEAD_DIM - 1, SEL_TQ), BF16)], axis=3)
        o_slc = _slc(qkv, selm, kaug, vt, bias_s, B, S)
        o_a = _banded(qkv, bias_a, a_sinks[l], 0, COL_AK, COL_AV, A_WINDOW, B, S)
        o_win = _banded(qkv, bias_w, None, 1, COL_BKW, COL_BVW, B_WINDOW, B, S)

        wr = jnp.concatenate([w_group[l], w_expert[l],
                              jnp.zeros((D, 128 - N_GROUPS - N_EXPERTS), F32)], axis=1)
        wrh = wr.astype(BF16)
        wrl = (wr - wrh.astype(F32)).astype(BF16)
        br = jnp.concatenate([b_group[l], b_expert[l],
                              jnp.zeros((128 - N_GROUPS - N_EXPERTS,), F32)]).reshape(1, 128)
        h, rinfo = _outproj(h, o_a, o_cmp, o_slc, o_win, gate, ex,
                            w_br_a[l].astype(BF16), w_br_b[l].astype(BF16), w_out[l].astype(BF16),
                            norm_ffn[l].reshape(1, D), wrh, wrl, br)

        blk_e, nused, gtok, gdst, wrow = _route(rinfo, T)
        moe = _moe(h, blk_e, nused, gtok, gdst, wrow, norm_ffn[l].reshape(1, D),
                   w1[l].astype(BF16), w3[l].astype(BF16), w2[l].astype(BF16))

    return _final(h, moe, norm_final.reshape(1, D)).reshape(B, S, D)
```

```python
import functools
import math

import numpy as np
import jax
import jax.numpy as jnp
from jax import lax
from jax.experimental import pallas as pl
from jax.experimental.pallas import tpu as pltpu

F32 = jnp.float32
BF16 = jnp.bfloat16
I32 = jnp.int32

D_MODEL = 1024
HEAD_DIM = 64
N_HEADS = 8
N_KV = 2
N_REP = 4
A_WINDOW = 128
B_WINDOW = 512
ATT_BLOCK = 128
CMP_LEN = 32
CMP_STRIDE = 16
CMP_HIDDEN = 256
SEL_BLOCK = 64
SEL_TOPN = 16
N_BUCKETS = 32
MAX_EXACT = 16
MAX_DIST = 128
N_GROUPS = 4
EXPERTS_PER_GROUP = 8
N_EXPERTS = 32
EXPERT_FF = 512
RMS_EPS = 1e-5
NEG = -1e30
FORCE = 1e9
SEL_MASK = -1e9
TAKEN = -3e38

QKV_W = 2048
GATE_W = 2176
BG_OFF = 2048
COL_AK, COL_AV, COL_BKC, COL_BVC, COL_BKS, COL_BVS, COL_BKW, COL_BVW = 8, 9, 10, 11, 12, 13, 14, 15

ROW_TILE = 512
BAND_TQ = 512
SEL_TQ = 256
MOE_TM = 512
DMA_UNROLL = 16
GATHER_BUFS = 3
ROW_CHUNKS = D_MODEL // 128
VMEM_LIMIT = 56 * 1024 * 1024

_NT = (((1,), (1,)), ((), ()))


def _cparams(sem):
    return pltpu.CompilerParams(dimension_semantics=sem, vmem_limit_bytes=VMEM_LIMIT)


def _rms(h, g):
    ms = jnp.mean(h * h, axis=-1, keepdims=True)
    return (h * lax.rsqrt(ms + RMS_EPS)) * g


def _sigmoid(z):
    return 1.0 / (1.0 + jnp.exp(-z))


def _stack_heads(ref, r0, r1, c0):
    return jnp.concatenate(
        [ref[r0:r1, c0 + r * HEAD_DIM:c0 + (r + 1) * HEAD_DIM] for r in range(N_REP)], axis=0)


def _rows_from_tiles(*refs):
    n = refs[0].shape[0] // ROW_CHUNKS
    return jnp.concatenate(
        [sum(r[pl.ds(c, n, stride=ROW_CHUNKS), :] for r in refs) for c in range(ROW_CHUNKS)], axis=1)


def _rows_to_tiles(ref, x):
    n = x.shape[0]
    for c in range(ROW_CHUNKS):
        ref[pl.ds(c, n, stride=ROW_CHUNKS), :] = x[:, c * 128:(c + 1) * 128]


def _unstack_heads(o, n):
    return jnp.concatenate([o[r * n:(r + 1) * n] for r in range(N_REP)], axis=1)


def _inproj_body(with_moe, *refs):
    if with_moe:
        h_ref, m0_ref, m1_ref, g_ref, w_ref, hout_ref, qkv_ref, gate_ref = refs
        h = _rows_from_tiles(h_ref, m0_ref, m1_ref)
        hout_ref[...] = h
    else:
        h_ref, g_ref, w_ref, qkv_ref, gate_ref = refs
        h = h_ref[...]
    xb = _rms(h, g_ref[...]).astype(BF16)
    for c0 in range(0, QKV_W, 512):
        acc = jnp.dot(xb, w_ref[:, c0:c0 + 512], preferred_element_type=F32)
        qkv_ref[:, c0:c0 + 512] = acc.astype(BF16)
    for c0 in range(0, GATE_W, 512):
        c1 = min(c0 + 512, GATE_W)
        z = jnp.dot(xb, w_ref[:, QKV_W + c0:QKV_W + c1], preferred_element_type=F32)
        gate_ref[:, c0:c1] = _sigmoid(z).astype(BF16)


def _inproj(h, moe, gain, w_p):
    T = h.shape[0] if moe is None else h.shape[0] // ROW_CHUNKS
    tm = ROW_TILE
    nt = T // tm
    row = lambda i: (i, 0)
    const = lambda i: (0, 0)
    tile = (tm * ROW_CHUNKS, 128)
    in_specs = [pl.BlockSpec((tm, D_MODEL), row) if moe is None else pl.BlockSpec(tile, row)]
    args = [h]
    out_shape = []
    out_specs = []
    if moe is not None:
        in_specs += [pl.BlockSpec(tile, row), pl.BlockSpec(tile, lambda i: (i + nt, 0))]
        args += [moe, moe]
        out_shape.append(jax.ShapeDtypeStruct((T, D_MODEL), F32))
        out_specs.append(pl.BlockSpec((tm, D_MODEL), row))
    in_specs += [pl.BlockSpec((1, D_MODEL), const),
                 pl.BlockSpec((D_MODEL, QKV_W + GATE_W), const)]
    args += [gain, w_p]
    out_shape += [jax.ShapeDtypeStruct((T, QKV_W), BF16), jax.ShapeDtypeStruct((T, GATE_W), BF16)]
    out_specs += [pl.BlockSpec((tm, QKV_W), row), pl.BlockSpec((tm, GATE_W), row)]
    res = pl.pallas_call(
        functools.partial(_inproj_body, moe is not None),
        grid=(nt,), in_specs=in_specs, out_specs=out_specs, out_shape=out_shape,
        compiler_params=_cparams(("parallel",)), name="inproj")(*args)
    if moe is None:
        return h, res[0], res[1]
    return res[0], res[1], res[2]


def _compress_body(x_ref, pos_ref, w1_ref, w2_ref, o_ref):
    half = CMP_STRIDE * HEAD_DIM
    for g in range(N_KV):
        x = x_ref[0, 0, g].astype(F32)
        lo = (x + pos_ref[0, 0:1, :]).astype(BF16)
        hi = (x + pos_ref[0, 1:2, :]).astype(BF16)
        a = jnp.dot(lo, w1_ref[0, 0:half, :], preferred_element_type=F32)
        b = jnp.dot(hi, w1_ref[0, half:2 * half, :], preferred_element_type=F32)
        n = b.shape[0]
        hsum = a + pltpu.roll(b, n - 1, 0)
        hid = jax.nn.gelu(hsum, approximate=True).astype(BF16)
        o_ref[0, 0, g] = jnp.dot(hid, w2_ref[0], preferred_element_type=F32).astype(BF16)


def _compress(xr, pos, w1, w2):
    _, B, G, nr, _ = xr.shape
    return pl.pallas_call(
        _compress_body,
        grid=(2, B),
        in_specs=[pl.BlockSpec((1, 1, G, nr, CMP_STRIDE * HEAD_DIM), lambda k, b: (k, b, 0, 0, 0)),
                  pl.BlockSpec((1, 2, CMP_STRIDE * HEAD_DIM), lambda k, b: (k, 0, 0)),
                  pl.BlockSpec((1, CMP_LEN * HEAD_DIM, CMP_HIDDEN), lambda k, b: (k, 0, 0)),
                  pl.BlockSpec((1, CMP_HIDDEN, HEAD_DIM), lambda k, b: (k, 0, 0))],
        out_specs=pl.BlockSpec((1, 1, G, nr, HEAD_DIM), lambda k, b: (k, b, 0, 0, 0)),
        out_shape=jax.ShapeDtypeStruct((2, B, G, nr, HEAD_DIM), BF16),
        compiler_params=_cparams(("parallel", "parallel")), name="compress")(xr, pos, w1, w2)


def _cmpsel_body(q_ref, kc_ref, vc_ref, ovt_ref, ocmp_ref, selm_ref):
    i = pl.program_id(1)
    tq = q_ref.shape[0]
    nc = kc_ref.shape[3]
    ns = ovt_ref.shape[0]
    t0 = i * tq
    tcol = t0 + lax.broadcasted_iota(I32, (tq, 1), 0)
    ncol = lax.broadcasted_iota(I32, (1, nc), 1)
    cval = (ncol * CMP_STRIDE + (CMP_LEN - 1)) <= tcol
    cval4 = jnp.concatenate([cval] * N_REP, axis=0)
    jrow = lax.broadcasted_iota(I32, (ns, 1), 0)
    trow = t0 + lax.broadcasted_iota(I32, (1, tq), 1)
    tb = lax.shift_right_logical(trow, 6)
    forced = (jrow == 0) | (jrow == tb) | (jrow == tb - 1)
    causal = (jrow * SEL_BLOCK) <= trow
    n_sel = min(SEL_TOPN, ns)
    jrow_f = jrow.astype(F32)
    for g in range(N_KV):
        qs = _stack_heads(q_ref, 0, tq, g * N_REP * HEAD_DIM)
        s = lax.dot_general(qs, kc_ref[0, 0, g], _NT, preferred_element_type=F32)
        sm = jnp.where(cval4, s, NEG)
        m = jnp.max(sm, axis=-1, keepdims=True)
        e = jnp.where(cval4, jnp.exp(sm - m), 0.0)
        den = jnp.sum(e, axis=-1, keepdims=True)
        pc = e / jnp.where(den > 0.0, den, 1.0)
        o = jnp.dot(pc.astype(BF16), vc_ref[0, 0, g], preferred_element_type=F32)
        ocmp_ref[:, g * 256:(g + 1) * 256] = _unstack_heads(o, tq).astype(BF16)
        pcs = pc[0:tq] + pc[tq:2 * tq] + pc[2 * tq:3 * tq] + pc[3 * tq:4 * tq]
        hi = pcs.astype(BF16)
        lo = (pcs - hi.astype(F32)).astype(BF16)
        imp = (lax.dot_general(ovt_ref[...], hi, _NT, preferred_element_type=F32)
               + lax.dot_general(ovt_ref[...], lo, _NT, preferred_element_type=F32))
        imp = jnp.where(causal, jnp.where(forced, FORCE, imp), NEG)
        work = imp
        taken = jnp.zeros((ns, tq), F32)
        for _ in range(n_sel):
            top = jnp.max(work, axis=0, keepdims=True)
            first = jnp.min(jnp.where(work == top, jrow_f, float(ns)), axis=0, keepdims=True)
            hit = jrow_f == first
            taken = jnp.where(hit, 1.0, taken)
            work = jnp.where(hit, TAKEN, work)
        sel = (taken > 0.5) & (imp > NEG * 0.5)
        mt = jnp.where(sel, 0.0, SEL_MASK)
        if ns < 128:
            mt = jnp.concatenate([mt, jnp.zeros((128 - ns, tq), F32)], axis=0)
        selm_ref[:, g * 128:(g + 1) * 128] = mt.T.astype(BF16)


def _cmpsel(qkv, kcmp, vcmp, ovt, B, S):
    T = B * S
    tq = SEL_TQ
    nq = S // tq
    nc = kcmp.shape[3]
    ns = ovt.shape[0]
    return pl.pallas_call(
        _cmpsel_body,
        grid=(B, nq),
        in_specs=[pl.BlockSpec((tq, 512), lambda b, i: (b * nq + i, 1)),
                  pl.BlockSpec((1, 1, N_KV, nc, HEAD_DIM), lambda b, i: (0, b, 0, 0, 0)),
                  pl.BlockSpec((1, 1, N_KV, nc, HEAD_DIM), lambda b, i: (1, b, 0, 0, 0)),
                  pl.BlockSpec((ns, nc), lambda b, i: (0, 0))],
        out_specs=[pl.BlockSpec((tq, 512), lambda b, i: (b * nq + i, 0)),
                   pl.BlockSpec((tq, 256), lambda b, i: (b * nq + i, 0))],
        out_shape=[jax.ShapeDtypeStruct((T, 512), BF16), jax.ShapeDtypeStruct((T, 256), BF16)],
        compiler_params=_cparams(("parallel", "parallel")), name="cmpsel")(qkv, kcmp, vcmp, ovt)


V_ROWS = 80


def _slc_body(q_ref, selm_ref, ka_ref, vt_ref, bias_ref, o_ref, qa_scr, m_scr, acc_scr):
    i = pl.program_id(1)
    tq = q_ref.shape[0]
    prev = jnp.maximum(i - 1, 0)

    def update(s, vt):
        m_old = m_scr[...]
        m_new = jnp.maximum(m_old, jnp.max(s, axis=0, keepdims=True))
        alpha = jnp.exp(m_old - m_new)
        p = jnp.exp(s - m_new).astype(BF16)
        acc_scr[...] = alpha * acc_scr[...] + jnp.dot(vt, p, preferred_element_type=F32)
        m_scr[...] = m_new

    def scores(c, g):
        off = pl.multiple_of(c * tq, tq)
        kt = ka_ref[pl.ds(off, tq), g * 128:(g + 1) * 128]
        return lax.dot_general(kt, qa_scr[...], _NT, preferred_element_type=F32)

    for g in range(N_KV):
        sm = selm_ref[:, g * 128:g * 128 + SEL_BLOCK]
        for r in range(N_REP):
            c0 = g * 256 + r * HEAD_DIM
            qa_scr[r * tq:(r + 1) * tq, :] = jnp.concatenate([q_ref[:, c0:c0 + HEAD_DIM], sm], axis=1)
        m_scr[...] = jnp.full(m_scr.shape, NEG, F32)
        acc_scr[...] = jnp.zeros(acc_scr.shape, F32)
        update(scores(i, g) + bias_ref[g, tq:2 * tq, :], vt_ref[0, i, g])

        @pl.when(i > 0)
        def _():
            update(scores(prev, g) + bias_ref[g, 0:tq, :], vt_ref[0, prev, g])

        def far(c2, carry):
            c = 2 * c2
            off = pl.multiple_of(c * tq, 2 * tq)
            kt = ka_ref[pl.ds(off, 2 * tq), g * 128:(g + 1) * 128]
            s = lax.dot_general(kt, qa_scr[...], _NT, preferred_element_type=F32)
            update(s, jnp.concatenate([vt_ref[0, c, g], vt_ref[0, c + 1, g]], axis=1))
            return carry

        lax.fori_loop(0, lax.shift_right_logical(prev, 1), far, 0)

        @pl.when(lax.rem(prev, 2) == 1)
        def _():
            update(scores(prev - 1, g), vt_ref[0, prev - 1, g])
        acc = acc_scr[...]
        ot = acc[0:HEAD_DIM] / acc[HEAD_DIM:HEAD_DIM + 1]
        pad = jnp.zeros((128 - HEAD_DIM, tq), F32)
        heads = [jnp.concatenate([ot[:, r * tq:(r + 1) * tq], pad], axis=0).T[:, 0:HEAD_DIM]
                 for r in range(N_REP)]
        o_ref[:, g * 256:(g + 1) * 256] = jnp.concatenate(heads, axis=1).astype(BF16)


def _slc(qkv, selm, kaug, vt, bias_near, B, S):
    T = B * S
    tq = SEL_TQ
    nq = S // tq
    return pl.pallas_call(
        _slc_body,
        grid=(B, nq),
        in_specs=[pl.BlockSpec((tq, 512), lambda b, i: (b * nq + i, 1)),
                  pl.BlockSpec((tq, 256), lambda b, i: (b * nq + i, 0)),
                  pl.BlockSpec((S, 256), lambda b, i: (b, 0)),
                  pl.BlockSpec((1, nq, N_KV, V_ROWS, tq), lambda b, i: (b, 0, 0, 0, 0)),
                  pl.BlockSpec((N_KV, 2 * tq, N_REP * tq), lambda b, i: (0, 0, 0))],
        out_specs=pl.BlockSpec((tq, 512), lambda b, i: (b * nq + i, 0)),
        out_shape=jax.ShapeDtypeStruct((T, 512), BF16),
        scratch_shapes=[pltpu.VMEM((N_REP * tq, 128), BF16),
                        pltpu.VMEM((1, N_REP * tq), F32),
                        pltpu.VMEM((V_ROWS, N_REP * tq), F32)],
        compiler_params=_cparams(("parallel", "arbitrary")), name="slc")(qkv, selm, kaug, vt, bias_near)


def _band_body(npv, has_sink, *refs):
    if has_sink:
        q_ref, kp_ref, km_ref, vp_ref, vm_ref, bias_ref, sink_ref, o_ref = refs
    else:
        q_ref, kp_ref, km_ref, vp_ref, vm_ref, bias_ref, o_ref = refs
    i = pl.program_id(1)
    blk = ATT_BLOCK
    L = (npv + 1) * blk
    nsub = q_ref.shape[0] // blk
    col = lax.broadcasted_iota(I32, (1, L), 1)
    for g in range(N_KV):
        ks = slice(g * HEAD_DIM, (g + 1) * HEAD_DIM)
        kfull = jnp.concatenate([kp_ref[:, ks], km_ref[:, ks]], axis=0)
        vfull = jnp.concatenate([vp_ref[:, ks], vm_ref[:, ks]], axis=0)
        for sub in range(nsub):
            qs = _stack_heads(q_ref, sub * blk, (sub + 1) * blk, g * N_REP * HEAD_DIM)
            s = lax.dot_general(qs, kfull[sub * blk:sub * blk + L], _NT,
                                preferred_element_type=F32) + bias_ref[g]
            ncut = (npv - sub) * blk
            if ncut > 0:
                s = jnp.where(jnp.logical_and(col < ncut, i == 0), NEG, s)
            if has_sink:
                s = jnp.where(col == 0, sink_ref[g], s)
            m = jnp.max(s, axis=-1, keepdims=True)
            e = jnp.exp(s - m)
            den = jnp.sum(e, axis=-1, keepdims=True)
            if has_sink:
                e = jnp.where(col == 0, 0.0, e)
            o = jnp.dot(e.astype(BF16), vfull[sub * blk:sub * blk + L],
                        preferred_element_type=F32) / den
            o_ref[sub * blk:(sub + 1) * blk, g * 256:(g + 1) * 256] = _unstack_heads(o, blk).astype(BF16)


def _banded(qkv, bias, sinks, qcol, kcol, vcol, window, B, S):
    T = B * S
    tq = BAND_TQ
    nq = S // tq
    npv = -(-(window - 1) // ATT_BLOCK)
    pv = npv * ATT_BLOCK
    L = pv + ATT_BLOCK
    ratio = tq // pv
    prev_map = lambda c: (lambda b, i: (b * (S // pv) + jnp.maximum(i * ratio - 1, 0), c * (128 // 128)))
    main_map = lambda c: (lambda b, i: (b * nq + i, c))
    in_specs = [pl.BlockSpec((tq, 512), lambda b, i: (b * nq + i, qcol)),
                pl.BlockSpec((pv, 128), prev_map(kcol)), pl.BlockSpec((tq, 128), main_map(kcol)),
                pl.BlockSpec((pv, 128), prev_map(vcol)), pl.BlockSpec((tq, 128), main_map(vcol)),
                pl.BlockSpec((N_KV, N_REP * ATT_BLOCK, L), lambda b, i: (0, 0, 0))]
    args = [qkv, qkv, qkv, qkv, qkv, bias]
    if sinks is not None:
        assert window <= npv * ATT_BLOCK
        in_specs.append(pl.BlockSpec((N_KV, N_REP * ATT_BLOCK, L), lambda b, i: (0, 0, 0)))
        args.append(jnp.broadcast_to(sinks.astype(F32).reshape(N_KV, N_REP, 1, 1),
                                     (N_KV, N_REP, ATT_BLOCK, L)).reshape(N_KV, N_REP * ATT_BLOCK, L))
    return pl.pallas_call(
        functools.partial(_band_body, npv, sinks is not None),
        grid=(B, nq), in_specs=in_specs,
        out_specs=pl.BlockSpec((tq, 512), lambda b, i: (b * nq + i, 0)),
        out_shape=jax.ShapeDtypeStruct((T, 512), BF16),
        compiler_params=_cparams(("parallel", "parallel")),
        name="band_sink" if sinks is not None else "band_win")(*args)


def _out_body(h_ref, oa_ref, oc_ref, os_ref, ow_ref, gate_ref, ex_ref, wa_ref, wb_ref, wo_ref,
              gn_ref, wrh_ref, wrl_ref, br_ref, hout_ref, rinfo_ref):
    bgs = gate_ref[:, BG_OFF:BG_OFF + 128]
    ob = (jnp.dot(bgs, ex_ref[0], preferred_element_type=F32) * oc_ref[...].astype(F32)
          + jnp.dot(bgs, ex_ref[1], preferred_element_type=F32) * os_ref[...].astype(F32)
          + jnp.dot(bgs, ex_ref[2], preferred_element_type=F32) * ow_ref[...].astype(F32))
    ta = jnp.dot(oa_ref[...], wa_ref[...], preferred_element_type=F32)
    tb = jnp.dot(ob.astype(BF16), wb_ref[...], preferred_element_type=F32)
    merged = (gate_ref[:, 0:D_MODEL].astype(F32) * ta
              + gate_ref[:, D_MODEL:2 * D_MODEL].astype(F32) * tb)
    hn = h_ref[...] + jnp.dot(merged.astype(BF16), wo_ref[...], preferred_element_type=F32)
    _rows_to_tiles(hout_ref, hn)
    xn = _rms(hn, gn_ref[...])
    xh = xn.astype(BF16)
    xl = (xn - xh.astype(F32)).astype(BF16)
    logits = (jnp.dot(xh, wrh_ref[...], preferred_element_type=F32)
              + jnp.dot(xl, wrh_ref[...], preferred_element_type=F32)
              + jnp.dot(xh, wrl_ref[...], preferred_element_type=F32)) + br_ref[...]
    tm = logits.shape[0]
    lane = lax.broadcasted_iota(I32, (1, 128), 1).astype(F32)
    big = 1e9
    is_g = lane < N_GROUPS
    glog = jnp.where(is_g, logits, NEG)
    gmax = jnp.max(glog, axis=-1, keepdims=True)
    gsel = jnp.min(jnp.where(glog == gmax, lane, big), axis=-1, keepdims=True)
    gsum = jnp.sum(jnp.where(is_g, jnp.exp(logits - gmax), 0.0), axis=-1, keepdims=True)
    gw = 1.0 / gsum
    e_lo = N_GROUPS + gsel * EXPERTS_PER_GROUP
    in_g = (lane >= e_lo) & (lane < e_lo + EXPERTS_PER_GROUP)
    ev = jnp.where(in_g, logits, NEG)
    v1 = jnp.max(ev, axis=-1, keepdims=True)
    i1 = jnp.min(jnp.where(ev == v1, lane, big), axis=-1, keepdims=True)
    ev2 = jnp.where(lane == i1, NEG, ev)
    v2 = jnp.max(ev2, axis=-1, keepdims=True)
    i2 = jnp.min(jnp.where(ev2 == v2, lane, big), axis=-1, keepdims=True)
    d = jnp.exp(v2 - v1)
    p1 = 1.0 / (1.0 + d)
    p2 = d / (1.0 + d)
    lane8 = lax.broadcasted_iota(I32, (tm, 8), 1)
    rinfo_ref[...] = jnp.where(lane8 == 0, i1 - N_GROUPS,
                     jnp.where(lane8 == 1, i2 - N_GROUPS,
                     jnp.where(lane8 == 2, p1 * gw,
                     jnp.where(lane8 == 3, p2 * gw, 0.0))))


def _outproj(h, oa, oc, osl, ow, gate, ex, wa, wb, wo, gn, wrh, wrl, br):
    T = h.shape[0]
    tm = ROW_TILE
    row = lambda i: (i, 0)
    c2 = lambda i: (0, 0)
    c3 = lambda i: (0, 0, 0)
    return pl.pallas_call(
        _out_body,
        grid=(T // tm,),
        in_specs=[pl.BlockSpec((tm, D_MODEL), row),
                  pl.BlockSpec((tm, 512), row), pl.BlockSpec((tm, 512), row),
                  pl.BlockSpec((tm, 512), row), pl.BlockSpec((tm, 512), row),
                  pl.BlockSpec((tm, GATE_W), row),
                  pl.BlockSpec((3, 128, 512), c3),
                  pl.BlockSpec((512, D_MODEL), c2), pl.BlockSpec((512, D_MODEL), c2),
                  pl.BlockSpec((D_MODEL, D_MODEL), c2),
                  pl.BlockSpec((1, D_MODEL), c2),
                  pl.BlockSpec((D_MODEL, 128), c2), pl.BlockSpec((D_MODEL, 128), c2),
                  pl.BlockSpec((1, 128), c2)],
        out_specs=[pl.BlockSpec((tm * ROW_CHUNKS, 128), row), pl.BlockSpec((tm, 8), row)],
        out_shape=[jax.ShapeDtypeStruct((T * ROW_CHUNKS, 128), F32), jax.ShapeDtypeStruct((T, 8), F32)],
        compiler_params=_cparams(("parallel",)), name="outproj")(
            h, oa, oc, osl, ow, gate, ex, wa, wb, wo, gn, wrh, wrl, br)


def _moe_body(be_ref, nu_ref, gtok_ref, gtokn_ref, gtokn2_ref, gdst_ref, gdstp_ref, wrow_ref, gn_ref,
              w1_ref, w3_ref, w2_ref, h_hbm, out_hbm, xbuf, ybuf, gsem, ssem):
    i = pl.program_id(0)
    nu = nu_ref[0]
    tm = xbuf.shape[1] // ROW_CHUNKS
    slot = lax.rem(i, GATHER_BUFS)

    def tile_of(row):
        return pl.ds(pl.multiple_of(row * ROW_CHUNKS, ROW_CHUNKS), ROW_CHUNKS)

    def gather_copy(idx_ref, sl, r):
        t = idx_ref[0, 0, r]
        return pltpu.make_async_copy(h_hbm.at[tile_of(t)], xbuf.at[sl, tile_of(r)], gsem.at[sl])

    def scatter_copy(idx_ref, r):
        d = idx_ref[0, 0, r]
        return pltpu.make_async_copy(ybuf.at[tile_of(r)], out_hbm.at[tile_of(d)], ssem.at[0])

    def drain_scatter(idx_ref):
        def body(r, c):
            scatter_copy(idx_ref, r).wait()
            return c
        lax.fori_loop(0, tm, body, 0, unroll=DMA_UNROLL)

    def start_gather(idx_ref, sl):
        def body(j, c):
            gather_copy(idx_ref, sl, 2 * j).start(priority=0)
            gather_copy(idx_ref, sl, 2 * j + 1).start(priority=1)
            return c
        lax.fori_loop(0, tm // 2, body, 0, unroll=DMA_UNROLL // 2)

    @pl.when(i == 0)
    def _():
        start_gather(gtok_ref, 0)
        ybuf[...] = jnp.zeros(ybuf.shape, F32)
        nfill = tm * ROW_CHUNKS
        fill = pltpu.make_async_copy(ybuf, out_hbm.at[pl.ds(out_hbm.shape[0] - nfill, nfill)], ssem.at[0])
        fill.start()
        fill.wait()

    @pl.when(jnp.logical_and(i == 0, 1 < nu))
    def _():
        start_gather(gtokn_ref, 1)

    @pl.when(i + 2 < nu)
    def _():
        start_gather(gtokn2_ref, lax.rem(i + 2, GATHER_BUFS))

    @pl.when(i < nu)
    def _():
        def wbody(r, c):
            gather_copy(gtok_ref, slot, r).wait()
            return c
        lax.fori_loop(0, tm, wbody, 0, unroll=DMA_UNROLL)
        x = _rms(_rows_from_tiles(xbuf.at[slot]), gn_ref[...]).astype(BF16)
        h1 = jnp.dot(x, w1_ref[0], preferred_element_type=F32)
        h3 = jnp.dot(x, w3_ref[0], preferred_element_type=F32)
        act = (h1 * _sigmoid(h1) * h3).astype(BF16)
        y = jnp.dot(act, w2_ref[0], preferred_element_type=F32) * wrow_ref[...]

        @pl.when(i > 0)
        def _():
            drain_scatter(gdstp_ref)

        _rows_to_tiles(ybuf, y)

        def sbody(j, c):
            scatter_copy(gdst_ref, 2 * j).start(priority=0)
            scatter_copy(gdst_ref, 2 * j + 1).start(priority=1)
            return c
        lax.fori_loop(0, tm // 2, sbody, 0, unroll=DMA_UNROLL // 2)

        @pl.when(i == nu - 1)
        def _():
            drain_scatter(gdst_ref)


def _moe(h, blk_e, nused, gtok, gdst, wrow, gn, w1, w3, w2):
    T = h.shape[0] // ROW_CHUNKS
    tm = MOE_TM
    nblk = gtok.shape[0]
    grid_spec = pltpu.PrefetchScalarGridSpec(
        num_scalar_prefetch=2,
        grid=(nblk,),
        in_specs=[pl.BlockSpec((1, 1, tm), lambda i, be, nu: (i, 0, 0), memory_space=pltpu.SMEM),
                  pl.BlockSpec((1, 1, tm), lambda i, be, nu: (jnp.minimum(i + 1, nblk - 1), 0, 0),
                               memory_space=pltpu.SMEM),
                  pl.BlockSpec((1, 1, tm), lambda i, be, nu: (jnp.minimum(i + 2, nblk - 1), 0, 0),
                               memory_space=pltpu.SMEM),
                  pl.BlockSpec((1, 1, tm), lambda i, be, nu: (i, 0, 0), memory_space=pltpu.SMEM),
                  pl.BlockSpec((1, 1, tm), lambda i, be, nu: (jnp.maximum(i - 1, 0), 0, 0),
                               memory_space=pltpu.SMEM),
                  pl.BlockSpec((tm, 1), lambda i, be, nu: (i, 0)),
                  pl.BlockSpec((1, D_MODEL), lambda i, be, nu: (0, 0)),
                  pl.BlockSpec((1, D_MODEL, EXPERT_FF), lambda i, be, nu: (be[i], 0, 0)),
                  pl.BlockSpec((1, D_MODEL, EXPERT_FF), lambda i, be, nu: (be[i], 0, 0)),
                  pl.BlockSpec((1, EXPERT_FF, D_MODEL), lambda i, be, nu: (be[i], 0, 0)),
                  pl.BlockSpec(memory_space=pl.ANY)],
        out_specs=pl.BlockSpec(memory_space=pl.ANY),
        scratch_shapes=[pltpu.VMEM((GATHER_BUFS, tm * ROW_CHUNKS, 128), F32),
                        pltpu.VMEM((tm * ROW_CHUNKS, 128), F32),
                        pltpu.SemaphoreType.DMA((GATHER_BUFS,)),
                        pltpu.SemaphoreType.DMA((1,))])
    return pl.pallas_call(
        _moe_body, grid_spec=grid_spec,
        out_shape=jax.ShapeDtypeStruct(((2 * T + tm) * ROW_CHUNKS, 128), F32),
        compiler_params=_cparams(("arbitrary",)), name="moe")(
            blk_e, nused, gtok, gtok, gtok, gdst, gdst, wrow, gn, w1, w3, w2, h)


def _route(rinfo, T):
    tm = MOE_TM
    A = 2 * T
    e_flat = rinfo[:, 0:2].astype(I32).T.reshape(A)
    w_flat = rinfo[:, 2:4].T.reshape(A)
    onehot = (e_flat[:, None] == jnp.arange(N_EXPERTS, dtype=I32)[None, :]).astype(I32)
    ck = 256
    tri = jnp.asarray(np.tril(np.ones((ck, ck), np.float32)), BF16)
    within = jnp.einsum('ij,cjk->cik', tri, onehot.astype(BF16).reshape(A // ck, ck, N_EXPERTS),
                        preferred_element_type=F32)
    tot = within[:, -1, :]
    csum = (within + (jnp.cumsum(tot, axis=0) - tot)[:, None, :]).reshape(A, N_EXPERTS).astype(I32)
    counts = csum[-1]
    padded = (counts + tm - 1) // tm * tm
    pend = jnp.cumsum(padded)
    pstart = pend - padded
    dest = jnp.sum(onehot * (csum - 1 + pstart[None, :]), axis=1)
    P = A + N_EXPERTS * tm
    nblk = P // tm
    a = jnp.arange(A, dtype=I32)
    tok = jnp.where(a >= T, a - T, a)
    upd = jnp.stack([tok, a, lax.bitcast_convert_type(w_flat, I32)], axis=1)
    base = jnp.stack([jnp.zeros((P,), I32), A + jnp.arange(P, dtype=I32) % tm, jnp.zeros((P,), I32)], axis=1)
    packed = base.at[dest].set(upd, unique_indices=True)
    gtok, gdst = packed[:, 0], packed[:, 1]
    wrow = lax.bitcast_convert_type(packed[:, 2], F32)
    starts = jnp.arange(nblk, dtype=I32) * tm
    blk_e = jnp.minimum(jnp.sum((pend[None, :] <= starts[:, None]).astype(I32), axis=1),
                        N_EXPERTS - 1)
    nused = (pend[-1] // tm).astype(I32).reshape(1)
    return blk_e, nused, gtok.reshape(nblk, 1, tm), gdst.reshape(nblk, 1, tm), wrow.reshape(P, 1)


def _final_body(h_ref, m0_ref, m1_ref, g_ref, o_ref):
    o_ref[...] = _rms(_rows_from_tiles(h_ref, m0_ref, m1_ref), g_ref[...])


def _final(h, moe, gain):
    T = h.shape[0] // ROW_CHUNKS
    tm = ROW_TILE
    nt = T // tm
    row = lambda i: (i, 0)
    return pl.pallas_call(
        _final_body, grid=(nt,),
        in_specs=[pl.BlockSpec((tm * ROW_CHUNKS, 128), row),
                  pl.BlockSpec((tm * ROW_CHUNKS, 128), row),
                  pl.BlockSpec((tm * ROW_CHUNKS, 128), lambda i: (i + nt, 0)),
                  pl.BlockSpec((1, D_MODEL), lambda i: (0, 0))],
        out_specs=pl.BlockSpec((tm, D_MODEL), row),
        out_shape=jax.ShapeDtypeStruct((T, D_MODEL), F32),
        compiler_params=_cparams(("parallel",)), name="final_norm")(h, moe, moe, gain)


def _bucket_table():
    n = np.arange(MAX_DIST)
    nf = np.maximum(n, 1).astype(np.float64)
    large = MAX_EXACT + (np.log(nf / MAX_EXACT) / math.log(MAX_DIST / MAX_EXACT)
                         * (N_BUCKETS - MAX_EXACT)).astype(np.int64)
    large = np.minimum(large, N_BUCKETS - 1)
    return np.where(n < MAX_EXACT, n, large).astype(np.int32)


def _toeplitz(vals, lo, window, nq, nk, off):
    m = nq + nk
    k = np.arange(m)
    d = off - np.where(k < nk, k, k - m)
    ok = (d >= lo) & (d < window)
    u = jnp.where(ok[None], vals[:, np.clip(d, 0, MAX_DIST - 1)], NEG)
    flat = jnp.tile(u, (1, nq))[:, :nq * (m - 1)]
    return flat.reshape(vals.shape[0], nq, m - 1)[:, :, :nk]


def _band_bias(bias_d, window, npv):
    L = (npv + 1) * ATT_BLOCK
    tile = _toeplitz(bias_d, 0, window, ATT_BLOCK, L, npv * ATT_BLOCK)
    return tile.reshape(N_KV, N_REP * ATT_BLOCK, L)


def _slc_bias(bias_d):
    tq = SEL_TQ
    rel = bias_d - bias_d[:, MAX_DIST - 1:MAX_DIST]
    tile = _toeplitz(rel, 0, 1 << 30, tq, 2 * tq, tq)
    return jnp.transpose(tile.reshape(N_KV, N_REP * tq, 2 * tq), (0, 2, 1))


def _prep_w_in(w):
    s = 0.125
    parts = [w[:, 0:512] * s, w[:, 768:1280] * s, w[:, 512:768], w[:, 1280:2048],
             w[:, 2072:4120], w[:, 2048:2072], jnp.zeros((D_MODEL, GATE_W - 2048 - 24), w.dtype)]
    return jnp.concatenate(parts, axis=1).astype(BF16)


def _gate_expand():
    ex = np.zeros((3, 128, 512), np.float32)
    for c in range(3):
        for h in range(N_HEADS):
            ex[c, h * 3 + c, h * HEAD_DIM:(h + 1) * HEAD_DIM] = 1.0
    return jnp.asarray(ex, BF16)


def _overlap_t(S):
    nc = (S - CMP_LEN) // CMP_STRIDE + 1
    ns = S // SEL_BLOCK
    cstart = np.arange(nc) * CMP_STRIDE
    sstart = np.arange(ns) * SEL_BLOCK
    ov = ((cstart[:, None] < sstart[None, :] + SEL_BLOCK)
          & (cstart[:, None] + CMP_LEN > sstart[None, :])).astype(np.float32)
    ovt = np.zeros((ns, S // CMP_STRIDE), np.float32)
    ovt[:, :nc] = ov.T
    return jnp.asarray(ovt, BF16)


def _block_onehot(S):
    e = (np.arange(S)[:, None] // SEL_BLOCK == np.arange(SEL_BLOCK)[None, :]).astype(np.float32)
    return jnp.asarray(e, BF16)


def kernel(x, rel_bias, norm_mix, w_in, a_sinks, cmp_pos_k, cmp_w1_k, cmp_w2_k, cmp_pos_v, cmp_w1_v,
           cmp_w2_v, w_br_a, w_br_b, w_out, norm_ffn, w_group, b_group, w_expert, b_expert, w1, w3, w2,
           norm_final):
    B, S, D = x.shape
    T = B * S
    depth = w_in.shape[0]
    assert D == D_MODEL and S % BAND_TQ == 0 and S // SEL_BLOCK <= SEL_BLOCK and T % ROW_TILE == 0

    bias_d = rel_bias[_bucket_table()].T.astype(F32)
    bias_a = _band_bias(bias_d[:N_HEADS], A_WINDOW, 1)
    bias_w = _band_bias(bias_d[N_HEADS:], B_WINDOW, 4)
    bias_s = _slc_bias(bias_d[N_HEADS:])
    ex = _gate_expand()
    ovt = _overlap_t(S)
    onehot = jnp.broadcast_to(_block_onehot(S)[None, :, None, :], (B, S, N_KV, SEL_BLOCK))
    half = CMP_STRIDE * HEAD_DIM

    h = x.reshape(T, D)
    moe = None
    for l in range(depth):
        h, qkv, gate = _inproj(h, moe, norm_mix[l].reshape(1, D), _prep_w_in(w_in[l]))

        def rows16(c):
            t = qkv[:, c * 128:(c + 1) * 128].reshape(B, S, N_KV, HEAD_DIM)
            return jnp.transpose(t, (0, 2, 1, 3)).reshape(B, N_KV, S // CMP_STRIDE, half)

        xr = jnp.stack([rows16(COL_BKC), rows16(COL_BVC)])
        pos = jnp.stack([cmp_pos_k[l].reshape(2, half), cmp_pos_v[l].reshape(2, half)])
        cw1 = jnp.stack([cmp_w1_k[l], cmp_w1_v[l]]).astype(BF16)
        cw2 = jnp.stack([cmp_w2_k[l], cmp_w2_v[l]]).astype(BF16)
        kv_cmp = _compress(xr, pos, cw1, cw2)

        o_cmp, selm = _cmpsel(qkv, kv_cmp, kv_cmp, ovt, B, S)
        ks = qkv[:, COL_BKS * 128:(COL_BKS + 1) * 128].reshape(B, S, N_KV, HEAD_DIM)
        kaug = jnp.concatenate([ks, onehot], axis=-1).reshape(T, N_KV * 128)
        nch = S // SEL_TQ
        vs = qkv[:, COL_BVS * 128:(COL_BVS + 1) * 128].reshape(B, nch, SEL_TQ, N_KV, HEAD_DIM)
        vt = jnp.concatenate([jnp.transpose(vs, (0, 1, 3, 4, 2)),
                              jnp.ones((B, nch, N_KV, 1, SEL_TQ), BF16),
                              jnp.zeros((B, nch, N_KV, V_ROWS - HEAD_DIM - 1, SEL_TQ), BF16)], axis=3)
        o_slc = _slc(qkv, selm, kaug, vt, bias_s, B, S)
        o_a = _banded(qkv, bias_a, a_sinks[l], 0, COL_AK, COL_AV, A_WINDOW, B, S)
        o_win = _banded(qkv, bias_w, None, 1, COL_BKW, COL_BVW, B_WINDOW, B, S)

        wr = jnp.concatenate([w_group[l], w_expert[l],
                              jnp.zeros((D, 128 - N_GROUPS - N_EXPERTS), F32)], axis=1)
        wrh = wr.astype(BF16)
        wrl = (wr - wrh.astype(F32)).astype(BF16)
        br = jnp.concatenate([b_group[l], b_expert[l],
                              jnp.zeros((128 - N_GROUPS - N_EXPERTS,), F32)]).reshape(1, 128)
        h, rinfo = _outproj(h, o_a, o_cmp, o_slc, o_win, gate, ex,
                            w_br_a[l].astype(BF16), w_br_b[l].astype(BF16), w_out[l].astype(BF16),
                            norm_ffn[l].reshape(1, D), wrh, wrl, br)

        blk_e, nused, gtok, gdst, wrow = _route(rinfo, T)
        moe = _moe(h, blk_e, nused, gtok, gdst, wrow, norm_ffn[l].reshape(1, D),
                   w1[l].astype(BF16), w3[l].astype(BF16), w2[l].astype(BF16))

    return _final(h, moe, norm_final.reshape(1, D)).reshape(B, S, D)
```
